```python
import jax, jax.numpy as jnp
from jax import lax
import numpy as np

D_MODEL = 1024
BATCH = 1
SEQ = 16384
DEPTH = 2

D_MIX = D_MODEL
W_A = D_MIX // 4
W_B = D_MIX // 4
W_C = D_MIX // 4
W_D = D_MIX // 4
SHORT_CONV = 3
POOL_WINDOWS = (2, 4, 8, 16)
N_POOL_GROUPS = len(POOL_WINDOWS)
POOL_GROUP_DIM = W_B // N_POOL_GROUPS
CONF_WIDTH = 31
SGU_CHUNK = 128
N_HEADS_D = 4
SGU_HEAD_DIM = W_D // N_HEADS_D
IN_SIZES = (W_A, W_A, W_A, W_A,
            W_B, W_B,
            W_C, W_C, W_C,
            W_D, W_D, W_D)
D_IN = sum(IN_SIZES)
EPS = 1e-6

kernel_name = "hybrid_parallel_conv_pool_conformer_sgu"


def rms_norm(x, g):
    xf = x.astype(jnp.float32)
    y = xf * lax.rsqrt(jnp.mean(xf * xf, axis=-1, keepdims=True) + EPS)
    return (y * g.astype(jnp.float32)).astype(x.dtype)


def layer_norm(x, g, b):
    xf = x.astype(jnp.float32)
    mu = jnp.mean(xf, axis=-1, keepdims=True)
    var = jnp.mean(jnp.square(xf - mu), axis=-1, keepdims=True)
    y = (xf - mu) * lax.rsqrt(var + EPS)
    return (y * g.astype(jnp.float32) + b.astype(jnp.float32)).astype(x.dtype)


def causal_dwconv(x, w):
    k = w.shape[0]
    xp = jnp.pad(x, ((0, 0), (k - 1, 0), (0, 0)))
    return lax.conv_general_dilated(
        xp, w[:, None, :].astype(x.dtype), window_strides=(1,), padding='VALID',
        dimension_numbers=('NWC', 'WIO', 'NWC'), feature_group_count=x.shape[-1])


def short_conv_mixer(b_gate, c_gate, xt, w_conv):
    return b_gate * causal_dwconv(c_gate * xt, w_conv)


def multiscale_pool_mixer(p, w_pool, pool_scale):
    bsz, t_len, _ = p.shape
    pf = p.astype(jnp.float32)
    t_idx = jnp.arange(t_len)
    outs = []
    for grp, w in zip(jnp.split(pf, N_POOL_GROUPS, axis=-1), POOL_WINDOWS):
        cs = jnp.pad(jnp.cumsum(grp, axis=1), ((0, 0), (w, 0), (0, 0)))
        wsum = cs[:, w:] - cs[:, :t_len]
        count = jnp.minimum(t_idx + 1, w).astype(jnp.float32)[None, :, None]
        outs.append(wsum / count - grp)
    pooled = jnp.stack(outs, axis=2).astype(p.dtype)
    y = jnp.einsum('btgc,gcd->btgd', pooled, w_pool).reshape(bsz, t_len, W_B)
    return y * pool_scale


def conformer_conv_mixer(a, gl, w_dw, b_dw, ln_g, ln_b, w_pw2, b_pw2):
    h = a * jax.nn.sigmoid(gl)
    h = causal_dwconv(h, w_dw) + b_dw
    h = jax.nn.silu(layer_norm(h, ln_g, ln_b))
    return h @ w_pw2 + b_pw2


def chunked_sgu_mixer(u, v, ln_g, ln_b, w_s, b_s):
    u = jax.nn.gelu(u)
    v = layer_norm(jax.nn.gelu(v), ln_g, ln_b)
    bsz, t_len, _ = v.shape
    n_chunks = t_len // SGU_CHUNK
    vc = v.reshape(bsz, n_chunks, SGU_CHUNK, N_HEADS_D, SGU_HEAD_DIM)
    mask = jnp.tril(jnp.ones((SGU_CHUNK, SGU_CHUNK), dtype=bool))
    ws = jnp.where(mask[None], w_s, 0)
    mixed = jnp.einsum('hst,bnthc->bnshc', ws, vc) + b_s.T[None, None, :, :, None]
    return u * mixed.reshape(bsz, t_len, W_D)


def setup_inputs(seed: int = 0) -> dict:
    key = jax.random.key(seed)
    ks = jax.random.split(key, 24)
    f32 = jnp.float32
    nrm = lambda k, shape, s: jax.random.normal(k, shape, f32) * s
    return {
        "x": nrm(ks[0], (BATCH, SEQ, D_MODEL), 1.0),
        "c": nrm(ks[1], (BATCH, D_MODEL), 1.0),
        "norm_g": 1.0 + nrm(ks[2], (DEPTH, D_MODEL), 0.02),
        "w_ada": nrm(ks[3], (DEPTH, D_MODEL, 3 * D_MODEL), 0.5 * D_MODEL ** -0.5),
        "b_ada": nrm(ks[4], (DEPTH, 3 * D_MODEL), 0.01),
        "w_in": nrm(ks[5], (DEPTH, D_MODEL, D_IN), D_MODEL ** -0.5),
        "w_conv_a": nrm(ks[6], (DEPTH, SHORT_CONV, W_A), SHORT_CONV ** -0.5),
        "w_pool": nrm(ks[7], (DEPTH, N_POOL_GROUPS, POOL_GROUP_DIM, POOL_GROUP_DIM), POOL_GROUP_DIM ** -0.5),
        "pool_scale": 1.0 + nrm(ks[8], (DEPTH, W_B), 0.02),
        "w_dw_c": nrm(ks[9], (DEPTH, CONF_WIDTH, W_C), CONF_WIDTH ** -0.5),
        "b_dw_c": nrm(ks[10], (DEPTH, W_C), 0.01),
        "ln_g_c": 1.0 + nrm(ks[11], (DEPTH, W_C), 0.02),
        "ln_b_c": nrm(ks[12], (DEPTH, W_C), 0.01),
        "w_pw2_c": nrm(ks[13], (DEPTH, W_C, W_C), W_C ** -0.5),
        "b_pw2_c": nrm(ks[14], (DEPTH, W_C), 0.01),
        "ln_g_d": 1.0 + nrm(ks[15], (DEPTH, W_D), 0.02),
        "ln_b_d": nrm(ks[16], (DEPTH, W_D), 0.01),
        "w_s_d": nrm(ks[17], (DEPTH, N_HEADS_D, SGU_CHUNK, SGU_CHUNK), 0.5 * SGU_CHUNK ** -0.5),
        "b_s_d": 1.0 + nrm(ks[18], (DEPTH, N_HEADS_D, SGU_CHUNK), 0.02),
        "w_out": nrm(ks[19], (DEPTH, D_MIX, D_MODEL), D_MIX ** -0.5),
        "final_g": 1.0 + nrm(ks[20], (D_MODEL,), 0.02),
    }


def reference(x, c, norm_g, w_ada, b_ada, w_in, w_conv_a, w_pool, pool_scale,
              w_dw_c, b_dw_c, ln_g_c, ln_b_c, w_pw2_c, b_pw2_c,
              ln_g_d, ln_b_d, w_s_d, b_s_d, w_out, final_g):
    split_points = list(np.cumsum(IN_SIZES)[:-1])
    c_act = jax.nn.silu(c)
    for l in range(DEPTH):
        mod = c_act @ w_ada[l] + b_ada[l]
        shift, scale, gate = jnp.split(mod, 3, axis=-1)
        h = rms_norm(x, norm_g[l]) * (1.0 + scale[:, None, :]) + shift[:, None, :]
        z = h @ w_in[l]
        (a_b, a_c, a_x, a_g, b_p, b_g, c_a, c_gl, c_g,
         d_u, d_v, d_g) = jnp.split(z, split_points, axis=-1)
        y_a = short_conv_mixer(a_b, a_c, a_x, w_conv_a[l]) * jax.nn.silu(a_g)
        y_b = multiscale_pool_mixer(b_p, w_pool[l], pool_scale[l]) * jax.nn.silu(b_g)
        y_c = conformer_conv_mixer(c_a, c_gl, w_dw_c[l], b_dw_c[l], ln_g_c[l], ln_b_c[l],
                                   w_pw2_c[l], b_pw2_c[l]) * jax.nn.silu(c_g)
        y_d = chunked_sgu_mixer(d_u, d_v, ln_g_d[l], ln_b_d[l], w_s_d[l], b_s_d[l]) * jax.nn.silu(d_g)
        y = jnp.concatenate([y_a, y_b, y_c, y_d], axis=-1) @ w_out[l]
        x = x + gate[:, None, :] * y
    return rms_norm(x, final_g)
```

```python
import functools
import math

import jax
import jax.numpy as jnp
from jax import lax
from jax.experimental import pallas as pl
from jax.experimental.pallas import tpu as pltpu

D_MODEL = 1024
SEQ = 16384
DEPTH = 2
GW = 256
D_IN = 12 * GW
SHORT_CONV = 3
CONF_WIDTH = 31
POOL_WINDOWS = (2, 4, 8, 16)
POOL_GROUP_DIM = GW // len(POOL_WINDOWS)
SGU_CHUNK = 128
N_HEADS_D = 4
SGU_HEAD_DIM = GW // N_HEADS_D
EPS = 1e-6

ROW_TILE = 256
HEAD_A = 8
HEAD_P = 8
HEAD_C = 32
VMEM_LIMIT_BYTES = 56 * 1024 * 1024

V_POOL_SCALE, V_B_DW, V_LN_G_C, V_LN_B_C, V_B_PW2, V_LN_G_D, V_LN_B_D = range(7)
N_VEC = 8
N_TAPS = 40
TAP_A0 = 0
TAP_C0 = 8


def _silu(x):
    hx = 0.5 * x
    return hx + hx * jnp.tanh(hx)


def _sigmoid(x):
    return 0.5 + 0.5 * jnp.tanh(0.5 * x)


def _gelu_tanh(x):
    c = math.sqrt(2.0 / math.pi)
    inner = c * (x + 0.044715 * (x * x * x))
    return 0.5 * x * (1.0 + jnp.tanh(inner))


def _layer_norm(x, g, b):
    mu = jnp.mean(x, axis=-1, keepdims=True)
    xc = x - mu
    var = jnp.mean(xc * xc, axis=-1, keepdims=True)
    return xc * lax.rsqrt(var + EPS) * g + b


def _rms_norm(x, g):
    ms = jnp.mean(x * x, axis=-1, keepdims=True)
    return (x * lax.rsqrt(ms + EPS)) * g


def _mod_kernel(c_ref, w_ref, b_ref, o_ref):
    ca = _silu(c_ref[...]).astype(jnp.bfloat16)
    o_ref[0] = jnp.dot(ca, w_ref[0].astype(jnp.bfloat16),
                       preferred_element_type=jnp.float32) + b_ref[0]


def _trunk_kernel(x_ref, mod_ref, norm_g_ref, final_g_ref, w_in_ref, w_out_ref, taps_ref, vecs_ref,
                  w_pool_ref, w_pw2_ref, w_s_ref, b_s_ref, o_ref,
                  cx_buf, p_buf, s2_buf, s4_buf, s8_buf, h_buf):
    i = pl.program_id(0)
    T = ROW_TILE

    @pl.when(i == 0)
    def _():
        cx_buf[:, 0:HEAD_A, :] = jnp.zeros((DEPTH, HEAD_A, GW), jnp.float32)
        for buf in (p_buf, s2_buf, s4_buf, s8_buf):
            buf[:, 0:HEAD_P, :] = jnp.zeros((DEPTH, HEAD_P, GW), jnp.float32)
        h_buf[:, 0:HEAD_C, :] = jnp.zeros((DEPTH, HEAD_C, GW), jnp.float32)

    lane = lax.broadcasted_iota(jnp.int32, (T, GW), 1)
    row = lax.broadcasted_iota(jnp.int32, (T, GW), 0) + i * T
    group = lane // POOL_GROUP_DIM
    window = jnp.where(group == 0, POOL_WINDOWS[0],
                       jnp.where(group == 1, POOL_WINDOWS[1],
                                 jnp.where(group == 2, POOL_WINDOWS[2], POOL_WINDOWS[3])))
    inv_count = 1.0 / jnp.minimum(row + 1, window).astype(jnp.float32)
    head_of_lane = lax.broadcasted_iota(jnp.int32, (SGU_CHUNK, GW), 1) // SGU_HEAD_DIM
    tri = (lax.broadcasted_iota(jnp.int32, (SGU_CHUNK, SGU_CHUNK), 0)
           >= lax.broadcasted_iota(jnp.int32, (SGU_CHUNK, SGU_CHUNK), 1))

    def vec(l, k):
        return vecs_ref[l, k:k + 1, :]

    def layer(l, x_cur):
        shift = mod_ref[l, 0:1, 0:D_MODEL]
        scale = mod_ref[l, 0:1, D_MODEL:2 * D_MODEL]
        gate = mod_ref[l, 0:1, 2 * D_MODEL:3 * D_MODEL]
        h = _rms_norm(x_cur, norm_g_ref[l:l + 1, :]) * (1.0 + scale) + shift
        z = jnp.dot(h.astype(jnp.bfloat16), w_in_ref[l], preferred_element_type=jnp.float32)

        def zs(k):
            return z[:, k * GW:(k + 1) * GW]

        a_b, a_c, a_x, a_g, b_p, b_g, c_a, c_gl, c_g, d_u, d_v, d_g = [zs(k) for k in range(12)]

        cx_buf[l, HEAD_A:HEAD_A + T, :] = a_c * a_x
        conv = None
        for j in range(SHORT_CONV):
            off = HEAD_A - (SHORT_CONV - 1) + j
            term = taps_ref[l, TAP_A0 + j:TAP_A0 + j + 1, :] * cx_buf[l, off:off + T, :]
            conv = term if conv is None else conv + term
        y_a = (a_b * conv) * _silu(a_g)
        cx_buf[l, 0:HEAD_A, :] = cx_buf[l, T:T + HEAD_A, :]

        p_buf[l, HEAD_P:HEAD_P + T, :] = b_p
        s2 = b_p + p_buf[l, HEAD_P - 1:HEAD_P - 1 + T, :]
        s2_buf[l, HEAD_P:HEAD_P + T, :] = s2
        s4 = s2 + s2_buf[l, HEAD_P - 2:HEAD_P - 2 + T, :]
        s4_buf[l, HEAD_P:HEAD_P + T, :] = s4
        s8 = s4 + s4_buf[l, HEAD_P - 4:HEAD_P - 4 + T, :]
        s8_buf[l, HEAD_P:HEAD_P + T, :] = s8
        s16 = s8 + s8_buf[l, HEAD_P - 8:HEAD_P - 8 + T, :]
        for buf in (p_buf, s2_buf, s4_buf, s8_buf):
            buf[l, 0:HEAD_P, :] = buf[l, T:T + HEAD_P, :]
        wsum = jnp.where(group == 0, s2, jnp.where(group == 1, s4, jnp.where(group == 2, s8, s16)))
        pooled = wsum * inv_count - b_p
        y_b = jnp.dot(pooled.astype(jnp.bfloat16), w_pool_ref[l], preferred_element_type=jnp.float32)
        y_b = (y_b * vec(l, V_POOL_SCALE)) * _silu(b_g)

        h_buf[l, HEAD_C:HEAD_C + T, :] = c_a * _sigmoid(c_gl)
        acc = None
        for j in range(CONF_WIDTH):
            off = HEAD_C - (CONF_WIDTH - 1) + j
            term = taps_ref[l, TAP_C0 + j:TAP_C0 + j + 1, :] * h_buf[l, off:off + T, :]
            acc = term if acc is None else acc + term
        h_buf[l, 0:HEAD_C, :] = h_buf[l, T:T + HEAD_C, :]
        hc = acc + vec(l, V_B_DW)
        hc = _silu(_layer_norm(hc, vec(l, V_LN_G_C), vec(l, V_LN_B_C)))
        y_c = jnp.dot(hc.astype(jnp.bfloat16), w_pw2_ref[l], preferred_element_type=jnp.float32)
        y_c = (y_c + vec(l, V_B_PW2)) * _silu(c_g)

        u = _gelu_tanh(d_u)
        v = _layer_norm(_gelu_tanh(d_v), vec(l, V_LN_G_D), vec(l, V_LN_B_D)).astype(jnp.bfloat16)
        mixed_chunks = []
        for ck in range(T // SGU_CHUNK):
            v_ck = v[ck * SGU_CHUNK:(ck + 1) * SGU_CHUNK, :]
            mixed = None
            for hd in range(N_HEADS_D):
                ws = jnp.where(tri, w_s_ref[l, hd], 0.0).astype(jnp.bfloat16)
                part = jnp.dot(ws, v_ck, preferred_element_type=jnp.float32)
                mixed = part if mixed is None else jnp.where(head_of_lane == hd, part, mixed)
            mixed_chunks.append(mixed + b_s_ref[l])
        mixed = jnp.concatenate(mixed_chunks, axis=0)
        y_d = (u * mixed) * _silu(d_g)

        y_cat = jnp.concatenate([y_a, y_b, y_c, y_d], axis=-1).astype(jnp.bfloat16)
        y = jnp.dot(y_cat, w_out_ref[l], preferred_element_type=jnp.float32)
        return x_cur + gate * y

    x_cur = x_ref[...]
    for l in range(DEPTH):
        x_cur = layer(l, x_cur)
    o_ref[...] = _rms_norm(x_cur, final_g_ref[...])


def _const_spec(shape):
    zeros = (0,) * len(shape)
    return pl.BlockSpec(shape, lambda i: zeros, pipeline_mode=pl.Buffered(1))


def kernel(x, c, norm_g, w_ada, b_ada, w_in, w_conv_a, w_pool, pool_scale, w_dw_c, b_dw_c, ln_g_c, ln_b_c, w_pw2_c, b_pw2_c, ln_g_d, ln_b_d, w_s_d, b_s_d, w_out, final_g):
    f32, bf16 = jnp.float32, jnp.bfloat16
    assert x.shape == (1, SEQ, D_MODEL) and c.shape == (1, D_MODEL)

    c8 = jnp.broadcast_to(c, (8, D_MODEL))
    n_col = 3
    mod = pl.pallas_call(
        _mod_kernel,
        grid=(DEPTH, n_col),
        in_specs=[pl.BlockSpec((8, D_MODEL), lambda l, j: (0, 0)),
                  pl.BlockSpec((1, D_MODEL, D_MODEL), lambda l, j: (l, 0, j)),
                  pl.BlockSpec((1, 1, D_MODEL), lambda l, j: (l, 0, j))],
        out_specs=pl.BlockSpec((1, 8, D_MODEL), lambda l, j: (l, 0, j)),
        out_shape=jax.ShapeDtypeStruct((DEPTH, 8, 3 * D_MODEL), f32),
        name="adaln_mod",
    )(c8, w_ada, b_ada.reshape(DEPTH, 1, 3 * D_MODEL))

    taps = jnp.zeros((DEPTH, N_TAPS, GW), f32)
    taps = taps.at[:, TAP_A0:TAP_A0 + SHORT_CONV].set(w_conv_a)
    taps = taps.at[:, TAP_C0:TAP_C0 + CONF_WIDTH].set(w_dw_c)
    vecs = jnp.stack([pool_scale, b_dw_c, ln_g_c, ln_b_c, b_pw2_c, ln_g_d, ln_b_d,
                      jnp.zeros_like(pool_scale)], axis=1)
    w_pool_bd = jnp.zeros((DEPTH, GW, GW), f32)
    for g in range(len(POOL_WINDOWS)):
        sl = slice(g * POOL_GROUP_DIM, (g + 1) * POOL_GROUP_DIM)
        w_pool_bd = w_pool_bd.at[:, sl, sl].set(w_pool[:, g])
    b_s_full = jnp.repeat(jnp.swapaxes(b_s_d, 1, 2), SGU_HEAD_DIM, axis=2)

    T = ROW_TILE
    out = pl.pallas_call(
        _trunk_kernel,
        grid=(SEQ // T,),
        in_specs=[
            pl.BlockSpec((T, D_MODEL), lambda i: (i, 0)),
            _const_spec((DEPTH, 8, 3 * D_MODEL)),
            _const_spec((DEPTH, D_MODEL)),
            _const_spec((1, D_MODEL)),
            _const_spec((DEPTH, D_MODEL, D_IN)),
            _const_spec((DEPTH, D_MODEL, D_MODEL)),
            _const_spec((DEPTH, N_TAPS, GW)),
            _const_spec((DEPTH, N_VEC, GW)),
            _const_spec((DEPTH, GW, GW)),
            _const_spec((DEPTH, GW, GW)),
            _const_spec((DEPTH, N_HEADS_D, SGU_CHUNK, SGU_CHUNK)),
            _const_spec((DEPTH, SGU_CHUNK, GW)),
        ],
        out_specs=pl.BlockSpec((T, D_MODEL), lambda i: (i, 0)),
        out_shape=jax.ShapeDtypeStruct((SEQ, D_MODEL), f32),
        scratch_shapes=[
            pltpu.VMEM((DEPTH, HEAD_A + T, GW), f32),
            pltpu.VMEM((DEPTH, HEAD_P + T, GW), f32),
            pltpu.VMEM((DEPTH, HEAD_P + T, GW), f32),
            pltpu.VMEM((DEPTH, HEAD_P + T, GW), f32),
            pltpu.VMEM((DEPTH, HEAD_P + T, GW), f32),
            pltpu.VMEM((DEPTH, HEAD_C + T, GW), f32),
        ],
        compiler_params=pltpu.CompilerParams(
            dimension_semantics=("arbitrary",),
            vmem_limit_bytes=VMEM_LIMIT_BYTES),
        name="hybrid_trunk",
    )(x.reshape(SEQ, D_MODEL), mod, norm_g, final_g.reshape(1, D_MODEL),
      w_in.astype(bf16), w_out.astype(bf16), taps, vecs,
      w_pool_bd.astype(bf16), w_pw2_c.astype(bf16), w_s_d, b_s_full)
    return out.reshape(1, SEQ, D_MODEL)
```

```python
import math

import jax
import jax.numpy as jnp
from jax import lax
from jax.experimental import pallas as pl
from jax.experimental.pallas import tpu as pltpu

D_MODEL = 1024
SEQ = 16384
DEPTH = 2
GW = 256
D_IN = 12 * GW
SHORT_CONV = 3
CONF_WIDTH = 31
POOL_WINDOWS = (2, 4, 8, 16)
POOL_GROUP_DIM = GW // len(POOL_WINDOWS)
SGU_CHUNK = 128
N_HEADS_D = 4
SGU_HEAD_DIM = GW // N_HEADS_D
EPS = 1e-6

LANES = 128
N_CT = GW // LANES
ROW_TILE = 256
HEAD_A = 8
HEAD_P = 8
HEAD_C = 32
VMEM_LIMIT_BYTES = 56 * 1024 * 1024

V_POOL_SCALE, V_B_DW, V_LN_G_C, V_LN_B_C, V_B_PW2, V_LN_G_D, V_LN_B_D = range(7)
N_VEC = 8
N_TAPS = 40
TAP_A0 = 0
TAP_C0 = 8


def _silu(x):
    hx = 0.5 * x
    return hx + hx * jnp.tanh(hx)


def _sigmoid(x):
    return 0.5 + 0.5 * jnp.tanh(0.5 * x)


def _gelu_tanh(x):
    c = math.sqrt(2.0 / math.pi)
    inner = c * (x + 0.044715 * (x * x * x))
    return 0.5 * x * (1.0 + jnp.tanh(inner))


def _layer_norm(x, g, b):
    mu = jnp.mean(x, axis=-1, keepdims=True)
    xc = x - mu
    var = jnp.mean(xc * xc, axis=-1, keepdims=True)
    return xc * lax.rsqrt(var + EPS) * g + b


def _rms_norm(x, g):
    ms = jnp.mean(x * x, axis=-1, keepdims=True)
    return (x * lax.rsqrt(ms + EPS)) * g


def _pack_bf16_rows(w):
    *lead, k, n = w.shape
    wb = w.astype(jnp.bfloat16).reshape(*lead, k // 2, 2, n)
    return lax.bitcast_convert_type(jnp.swapaxes(wb, -1, -2), jnp.uint32)


def _unpack_bf16_rows(w_u32):
    return pltpu.bitcast(w_u32, jnp.bfloat16)


def _mod_kernel(c_ref, w_ref, b_ref, o_ref):
    ca = _silu(c_ref[...]).astype(jnp.bfloat16)
    o_ref[0] = jnp.dot(ca, w_ref[0].astype(jnp.bfloat16),
                       preferred_element_type=jnp.float32) + b_ref[0]


def _trunk_kernel(x_ref, mod_ref, norm_g_ref, final_g_ref, w_in_ref, w_out_ref, taps_ref, vecs_ref,
                  w_pool_ref, w_pw2_ref, w_s_ref, b_s_ref, o_ref,
                  cx_buf, p_buf, s2_buf, s4_buf, s8_buf, h_buf):
    i = pl.program_id(0)
    T = ROW_TILE

    @pl.when(i == 0)
    def _():
        cx_buf[:, :, 0:HEAD_A, :] = jnp.zeros((DEPTH, N_CT, HEAD_A, LANES), jnp.float32)
        for buf in (p_buf, s2_buf):
            buf[:, :, 0:HEAD_P, :] = jnp.zeros((DEPTH, N_CT, HEAD_P, LANES), jnp.float32)
        for buf in (s4_buf, s8_buf):
            buf[:, 0:HEAD_P, :] = jnp.zeros((DEPTH, HEAD_P, LANES), jnp.float32)
        h_buf[:, :, 0:HEAD_C, :] = jnp.zeros((DEPTH, N_CT, HEAD_C, LANES), jnp.float32)

    lane = lax.broadcasted_iota(jnp.int32, (T, LANES), 1)
    left = lane < POOL_GROUP_DIM
    row1 = lax.broadcasted_iota(jnp.int32, (T, LANES), 0) + (i * T + 1)
    inv_count = [1.0 / jnp.minimum(row1, jnp.where(left, POOL_WINDOWS[2 * ct], POOL_WINDOWS[2 * ct + 1])
                                   ).astype(jnp.float32) for ct in range(N_CT)]
    head_of_lane = lax.broadcasted_iota(jnp.int32, (SGU_CHUNK, GW), 1) // SGU_HEAD_DIM
    tri = (lax.broadcasted_iota(jnp.int32, (SGU_CHUNK, SGU_CHUNK), 0)
           >= lax.broadcasted_iota(jnp.int32, (SGU_CHUNK, SGU_CHUNK), 1))

    def vec(l, k):
        return vecs_ref[l, k:k + 1, :]

    def shifted(buf, idx, head, back):
        return buf[idx + (slice(head - back, head - back + T), slice(None))]

    def carry(buf, idx, head):
        buf[idx + (slice(0, head), slice(None))] = buf[idx + (slice(T, T + head), slice(None))]

    def layer(l, x_cur):
        shift = mod_ref[l, 0:1, 0:D_MODEL]
        scale = mod_ref[l, 0:1, D_MODEL:2 * D_MODEL]
        gate = mod_ref[l, 0:1, 2 * D_MODEL:3 * D_MODEL]
        h = _rms_norm(x_cur, norm_g_ref[l:l + 1, :]) * (1.0 + scale) + shift
        z = jnp.dot(h.astype(jnp.bfloat16), _unpack_bf16_rows(w_in_ref[l]),
                    preferred_element_type=jnp.float32)

        def zs(k):
            return z[:, k * GW:(k + 1) * GW]

        a_b, a_c, a_x, a_g, b_p, b_g, c_a, c_gl, c_g, d_u, d_v, d_g = [zs(k) for k in range(12)]

        cx = a_c * a_x
        convs = []
        for ct in range(N_CT):
            lanes = slice(ct * LANES, (ct + 1) * LANES)
            cx_buf[l, ct, HEAD_A:HEAD_A + T, :] = cx[:, lanes]
            conv = None
            for j in range(SHORT_CONV):
                term = (taps_ref[l, TAP_A0 + j:TAP_A0 + j + 1, lanes]
                        * shifted(cx_buf, (l, ct), HEAD_A, SHORT_CONV - 1 - j))
                conv = term if conv is None else conv + term
            carry(cx_buf, (l, ct), HEAD_A)
            convs.append(conv)
        y_a = (a_b * jnp.concatenate(convs, axis=-1)) * _silu(a_g)

        pooled = []
        for ct in range(N_CT):
            p = b_p[:, ct * LANES:(ct + 1) * LANES]
            p_buf[l, ct, HEAD_P:HEAD_P + T, :] = p
            s2 = p + shifted(p_buf, (l, ct), HEAD_P, 1)
            s2_buf[l, ct, HEAD_P:HEAD_P + T, :] = s2
            s4 = s2 + shifted(s2_buf, (l, ct), HEAD_P, 2)
            carry(p_buf, (l, ct), HEAD_P)
            carry(s2_buf, (l, ct), HEAD_P)
            if ct == 0:
                wsum = jnp.where(left, s2, s4)
            else:
                s4_buf[l, HEAD_P:HEAD_P + T, :] = s4
                s8 = s4 + shifted(s4_buf, (l,), HEAD_P, 4)
                s8_buf[l, HEAD_P:HEAD_P + T, :] = s8
                s16 = s8 + shifted(s8_buf, (l,), HEAD_P, 8)
                carry(s4_buf, (l,), HEAD_P)
                carry(s8_buf, (l,), HEAD_P)
                wsum = jnp.where(left, s8, s16)
            pooled.append(wsum * inv_count[ct] - p)
        pooled = jnp.concatenate(pooled, axis=-1)
        y_b = jnp.dot(pooled.astype(jnp.bfloat16), _unpack_bf16_rows(w_pool_ref[l]),
                      preferred_element_type=jnp.float32)
        y_b = (y_b * vec(l, V_POOL_SCALE)) * _silu(b_g)

        hgl = c_a * _sigmoid(c_gl)
        accs = []
        for ct in range(N_CT):
            lanes = slice(ct * LANES, (ct + 1) * LANES)
            h_buf[l, ct, HEAD_C:HEAD_C + T, :] = hgl[:, lanes]
            acc = None
            for j in range(CONF_WIDTH):
                term = (taps_ref[l, TAP_C0 + j:TAP_C0 + j + 1, lanes]
                        * shifted(h_buf, (l, ct), HEAD_C, CONF_WIDTH - 1 - j))
                acc = term if acc is None else acc + term
            carry(h_buf, (l, ct), HEAD_C)
            accs.append(acc)
        hc = jnp.concatenate(accs, axis=-1) + vec(l, V_B_DW)
        hc = _silu(_layer_norm(hc, vec(l, V_LN_G_C), vec(l, V_LN_B_C)))
        y_c = jnp.dot(hc.astype(jnp.bfloat16), _unpack_bf16_rows(w_pw2_ref[l]),
                      preferred_element_type=jnp.float32)
        y_c = (y_c + vec(l, V_B_PW2)) * _silu(c_g)

        u = _gelu_tanh(d_u)
        v = _layer_norm(_gelu_tanh(d_v), vec(l, V_LN_G_D), vec(l, V_LN_B_D)).astype(jnp.bfloat16)
        mixed_chunks = []
        for ck in range(T // SGU_CHUNK):
            v_ck = v[ck * SGU_CHUNK:(ck + 1) * SGU_CHUNK, :]
            mixed = None
            for hd in range(N_HEADS_D):
                ws = jnp.where(tri, w_s_ref[l, hd], 0.0).astype(jnp.bfloat16)
                part = jnp.dot(ws, v_ck, preferred_element_type=jnp.float32)
                mixed = part if mixed is None else jnp.where(head_of_lane == hd, part, mixed)
            mixed_chunks.append(mixed + b_s_ref[l])
        mixed = jnp.concatenate(mixed_chunks, axis=0)
        y_d = (u * mixed) * _silu(d_g)

        y_cat = jnp.concatenate([y_a, y_b, y_c, y_d], axis=-1).astype(jnp.bfloat16)
        y = jnp.dot(y_cat, _unpack_bf16_rows(w_out_ref[l]), preferred_element_type=jnp.float32)
        return x_cur + gate * y

    x_cur = x_ref[...]
    for l in range(DEPTH):
        x_cur = layer(l, x_cur)
    o_ref[...] = _rms_norm(x_cur, final_g_ref[...])


def _const_spec(shape):
    zeros = (0,) * len(shape)
    return pl.BlockSpec(shape, lambda i: zeros, pipeline_mode=pl.Buffered(1))


def kernel(x, c, norm_g, w_ada, b_ada, w_in, w_conv_a, w_pool, pool_scale, w_dw_c, b_dw_c, ln_g_c, ln_b_c, w_pw2_c, b_pw2_c, ln_g_d, ln_b_d, w_s_d, b_s_d, w_out, final_g):
    f32 = jnp.float32
    assert x.shape == (1, SEQ, D_MODEL) and c.shape == (1, D_MODEL)
    assert POOL_WINDOWS == (2, 4, 8, 16) and N_CT == 2

    c8 = jnp.broadcast_to(c, (8, D_MODEL))
    n_col = 3
    mod = pl.pallas_call(
        _mod_kernel,
        grid=(DEPTH, n_col),
        in_specs=[pl.BlockSpec((8, D_MODEL), lambda l, j: (0, 0)),
                  pl.BlockSpec((1, D_MODEL, D_MODEL), lambda l, j: (l, 0, j)),
                  pl.BlockSpec((1, 1, D_MODEL), lambda l, j: (l, 0, j))],
        out_specs=pl.BlockSpec((1, 8, D_MODEL), lambda l, j: (l, 0, j)),
        out_shape=jax.ShapeDtypeStruct((DEPTH, 8, 3 * D_MODEL), f32),
        name="adaln_mod",
    )(c8, w_ada, b_ada.reshape(DEPTH, 1, 3 * D_MODEL))

    taps = jnp.zeros((DEPTH, N_TAPS, GW), f32)
    taps = taps.at[:, TAP_A0:TAP_A0 + SHORT_CONV].set(w_conv_a)
    taps = taps.at[:, TAP_C0:TAP_C0 + CONF_WIDTH].set(w_dw_c)
    vecs = jnp.stack([pool_scale, b_dw_c, ln_g_c, ln_b_c, b_pw2_c, ln_g_d, ln_b_d,
                      jnp.zeros_like(pool_scale)], axis=1)
    w_pool_bd = jnp.zeros((DEPTH, GW, GW), f32)
    for g in range(len(POOL_WINDOWS)):
        sl = slice(g * POOL_GROUP_DIM, (g + 1) * POOL_GROUP_DIM)
        w_pool_bd = w_pool_bd.at[:, sl, sl].set(w_pool[:, g])
    b_s_full = jnp.repeat(jnp.swapaxes(b_s_d, 1, 2), SGU_HEAD_DIM, axis=2)

    T = ROW_TILE
    out = pl.pallas_call(
        _trunk_kernel,
        grid=(SEQ // T,),
        in_specs=[
            pl.BlockSpec((T, D_MODEL), lambda i: (i, 0)),
            _const_spec((DEPTH, 8, 3 * D_MODEL)),
            _const_spec((DEPTH, D_MODEL)),
            _const_spec((1, D_MODEL)),
            _const_spec((DEPTH, D_MODEL // 2, D_IN)),
            _const_spec((DEPTH, D_MODEL // 2, D_MODEL)),
            _const_spec((DEPTH, N_TAPS, GW)),
            _const_spec((DEPTH, N_VEC, GW)),
            _const_spec((DEPTH, GW // 2, GW)),
            _const_spec((DEPTH, GW // 2, GW)),
            _const_spec((DEPTH, N_HEADS_D, SGU_CHUNK, SGU_CHUNK)),
            _const_spec((DEPTH, SGU_CHUNK, GW)),
        ],
        out_specs=pl.BlockSpec((T, D_MODEL), lambda i: (i, 0)),
        out_shape=jax.ShapeDtypeStruct((SEQ, D_MODEL), f32),
        scratch_shapes=[
            pltpu.VMEM((DEPTH, N_CT, HEAD_A + T, LANES), f32),
            pltpu.VMEM((DEPTH, N_CT, HEAD_P + T, LANES), f32),
            pltpu.VMEM((DEPTH, N_CT, HEAD_P + T, LANES), f32),
            pltpu.VMEM((DEPTH, HEAD_P + T, LANES), f32),
            pltpu.VMEM((DEPTH, HEAD_P + T, LANES), f32),
            pltpu.VMEM((DEPTH, N_CT, HEAD_C + T, LANES), f32),
        ],
        compiler_params=pltpu.CompilerParams(
            dimension_semantics=("arbitrary",),
            vmem_limit_bytes=VMEM_LIMIT_BYTES),
        name="hybrid_trunk",
    )(x.reshape(SEQ, D_MODEL), mod, norm_g, final_g.reshape(1, D_MODEL),
      _pack_bf16_rows(w_in), _pack_bf16_rows(w_out), taps, vecs,
      _pack_bf16_rows(w_pool_bd), _pack_bf16_rows(w_pw2_c), w_s_d, b_s_full)
    return out.reshape(1, SEQ, D_MODEL)
```

```python
import math

import jax
import jax.numpy as jnp
from jax import lax
from jax.experimental import pallas as pl
from jax.experimental.pallas import tpu as pltpu

D_MODEL = 1024
SEQ = 16384
DEPTH = 2
GW = 256
D_IN = 12 * GW
SHORT_CONV = 3
CONF_WIDTH = 31
POOL_WINDOWS = (2, 4, 8, 16)
POOL_GROUP_DIM = GW // len(POOL_WINDOWS)
SGU_CHUNK = 128
N_HEADS_D = 4
SGU_HEAD_DIM = GW // N_HEADS_D
EPS = 1e-6

LANES = 128
N_CT = GW // LANES
ROW_TILE = 256
HEAD_A = 8
HEAD_P = 8
HEAD_C = 32
VMEM_LIMIT_BYTES = 56 * 1024 * 1024
PACK_COLS = 512

V_POOL_SCALE, V_B_DW, V_LN_G_C, V_LN_B_C, V_B_PW2, V_LN_G_D, V_LN_B_D = range(7)
N_VEC = 8
N_TAPS = 40
TAP_A0 = 0
TAP_C0 = 8


def _silu(x):
    hx = 0.5 * x
    return hx + hx * jnp.tanh(hx)


def _sigmoid(x):
    return 0.5 + 0.5 * jnp.tanh(0.5 * x)


def _gelu_tanh(x):
    c = math.sqrt(2.0 / math.pi)
    inner = c * (x + 0.044715 * (x * x * x))
    return 0.5 * x * (1.0 + jnp.tanh(inner))


def _layer_norm(x, g, b):
    mu = jnp.mean(x, axis=-1, keepdims=True)
    xc = x - mu
    var = jnp.mean(xc * xc, axis=-1, keepdims=True)
    return xc * lax.rsqrt(var + EPS) * g + b


def _rms_norm(x, g):
    ms = jnp.mean(x * x, axis=-1, keepdims=True)
    return (x * lax.rsqrt(ms + EPS)) * g


def _pack_kernel(w_ref, o_ref):
    o_ref[0] = pltpu.bitcast(w_ref[0].astype(jnp.bfloat16), jnp.uint32)


def _pack_bf16_rows(w, name):
    depth, k, n = w.shape
    return pl.pallas_call(
        _pack_kernel,
        grid=(depth, n // PACK_COLS),
        in_specs=[pl.BlockSpec((1, k, PACK_COLS), lambda l, j: (l, 0, j))],
        out_specs=pl.BlockSpec((1, k // 2, PACK_COLS), lambda l, j: (l, 0, j)),
        out_shape=jax.ShapeDtypeStruct((depth, k // 2, n), jnp.uint32),
        name=name,
    )(w)


def _unpack_bf16_rows(w_u32):
    return pltpu.bitcast(w_u32, jnp.bfloat16)


def _mod_kernel(c_ref, w_ref, b_ref, o_ref):
    ca = _silu(c_ref[...]).astype(jnp.bfloat16)
    o_ref[0] = jnp.dot(ca, w_ref[0].astype(jnp.bfloat16),
                       preferred_element_type=jnp.float32) + b_ref[0]


def _trunk_kernel(x_ref, mod_ref, norm_g_ref, final_g_ref, w_in_ref, w_out_ref, taps_ref, vecs_ref,
                  w_pool_ref, w_pw2_ref, w_s_ref, b_s_ref, o_ref,
                  cx_buf, p_buf, s2_buf, s4_buf, s8_buf, h_buf):
    i = pl.program_id(0)
    T = ROW_TILE

    @pl.when(i == 0)
    def _():
        cx_buf[:, :, 0:HEAD_A, :] = jnp.zeros((DEPTH, N_CT, HEAD_A, LANES), jnp.float32)
        for buf in (p_buf, s2_buf):
            buf[:, :, 0:HEAD_P, :] = jnp.zeros((DEPTH, N_CT, HEAD_P, LANES), jnp.float32)
        for buf in (s4_buf, s8_buf):
            buf[:, 0:HEAD_P, :] = jnp.zeros((DEPTH, HEAD_P, LANES), jnp.float32)
        h_buf[:, :, 0:HEAD_C, :] = jnp.zeros((DEPTH, N_CT, HEAD_C, LANES), jnp.float32)

    lane = lax.broadcasted_iota(jnp.int32, (T, LANES), 1)
    left = lane < POOL_GROUP_DIM
    row1 = lax.broadcasted_iota(jnp.int32, (T, LANES), 0) + (i * T + 1)
    inv_count = [1.0 / jnp.minimum(row1, jnp.where(left, POOL_WINDOWS[2 * ct], POOL_WINDOWS[2 * ct + 1])
                                   ).astype(jnp.float32) for ct in range(N_CT)]
    head_of_lane = lax.broadcasted_iota(jnp.int32, (SGU_CHUNK, GW), 1) // SGU_HEAD_DIM
    tri = (lax.broadcasted_iota(jnp.int32, (SGU_CHUNK, SGU_CHUNK), 0)
           >= lax.broadcasted_iota(jnp.int32, (SGU_CHUNK, SGU_CHUNK), 1))

    def vec(l, k):
        return vecs_ref[l, k:k + 1, :]

    def shifted(buf, idx, head, back):
        return buf[idx + (slice(head - back, head - back + T), slice(None))]

    def carry(buf, idx, head):
        buf[idx + (slice(0, head), slice(None))] = buf[idx + (slice(T, T + head), slice(None))]

    def layer(l, x_cur):
        shift = mod_ref[l, 0:1, 0:D_MODEL]
        scale = mod_ref[l, 0:1, D_MODEL:2 * D_MODEL]
        gate = mod_ref[l, 0:1, 2 * D_MODEL:3 * D_MODEL]
        h = _rms_norm(x_cur, norm_g_ref[l:l + 1, :]) * (1.0 + scale) + shift
        z = jnp.dot(h.astype(jnp.bfloat16), _unpack_bf16_rows(w_in_ref[l]),
                    preferred_element_type=jnp.float32)

        def zs(k):
            return z[:, k * GW:(k + 1) * GW]

        a_b, a_c, a_x, a_g, b_p, b_g, c_a, c_gl, c_g, d_u, d_v, d_g = [zs(k) for k in range(12)]

        cx = a_c * a_x
        convs = []
        for ct in range(N_CT):
            lanes = slice(ct * LANES, (ct + 1) * LANES)
            cx_buf[l, ct, HEAD_A:HEAD_A + T, :] = cx[:, lanes]
            conv = None
            for j in range(SHORT_CONV):
                term = (taps_ref[l, TAP_A0 + j:TAP_A0 + j + 1, lanes]
                        * shifted(cx_buf, (l, ct), HEAD_A, SHORT_CONV - 1 - j))
                conv = term if conv is None else conv + term
            carry(cx_buf, (l, ct), HEAD_A)
            convs.append(conv)
        y_a = (a_b * jnp.concatenate(convs, axis=-1)) * _silu(a_g)

        pooled = []
        for ct in range(N_CT):
            p = b_p[:, ct * LANES:(ct + 1) * LANES]
            p_buf[l, ct, HEAD_P:HEAD_P + T, :] = p
            s2 = p + shifted(p_buf, (l, ct), HEAD_P, 1)
            s2_buf[l, ct, HEAD_P:HEAD_P + T, :] = s2
            s4 = s2 + shifted(s2_buf, (l, ct), HEAD_P, 2)
            carry(p_buf, (l, ct), HEAD_P)
            carry(s2_buf, (l, ct), HEAD_P)
            if ct == 0:
                wsum = jnp.where(left, s2, s4)
            else:
                s4_buf[l, HEAD_P:HEAD_P + T, :] = s4
                s8 = s4 + shifted(s4_buf, (l,), HEAD_P, 4)
                s8_buf[l, HEAD_P:HEAD_P + T, :] = s8
                s16 = s8 + shifted(s8_buf, (l,), HEAD_P, 8)
                carry(s4_buf, (l,), HEAD_P)
                carry(s8_buf, (l,), HEAD_P)
                wsum = jnp.where(left, s8, s16)
            pooled.append(wsum * inv_count[ct] - p)
        pooled = jnp.concatenate(pooled, axis=-1)
        y_b = jnp.dot(pooled.astype(jnp.bfloat16), w_pool_ref[l], preferred_element_type=jnp.float32)
        y_b = (y_b * vec(l, V_POOL_SCALE)) * _silu(b_g)

        hgl = c_a * _sigmoid(c_gl)
        accs = []
        for ct in range(N_CT):
            lanes = slice(ct * LANES, (ct + 1) * LANES)
            h_buf[l, ct, HEAD_C:HEAD_C + T, :] = hgl[:, lanes]
            acc = None
            for j in range(CONF_WIDTH):
                term = (taps_ref[l, TAP_C0 + j:TAP_C0 + j + 1, lanes]
                        * shifted(h_buf, (l, ct), HEAD_C, CONF_WIDTH - 1 - j))
                acc = term if acc is None else acc + term
            carry(h_buf, (l, ct), HEAD_C)
            accs.append(acc)
        hc = jnp.concatenate(accs, axis=-1) + vec(l, V_B_DW)
        hc = _silu(_layer_norm(hc, vec(l, V_LN_G_C), vec(l, V_LN_B_C)))
        y_c = jnp.dot(hc.astype(jnp.bfloat16), w_pw2_ref[l], preferred_element_type=jnp.float32)
        y_c = (y_c + vec(l, V_B_PW2)) * _silu(c_g)

        u = _gelu_tanh(d_u)
        v = _layer_norm(_gelu_tanh(d_v), vec(l, V_LN_G_D), vec(l, V_LN_B_D)).astype(jnp.bfloat16)
        mixed_chunks = []
        for ck in range(T // SGU_CHUNK):
            v_ck = v[ck * SGU_CHUNK:(ck + 1) * SGU_CHUNK, :]
            mixed = None
            for hd in range(N_HEADS_D):
                ws = jnp.where(tri, w_s_ref[l, hd], 0.0).astype(jnp.bfloat16)
                part = jnp.dot(ws, v_ck, preferred_element_type=jnp.float32)
                mixed = part if mixed is None else jnp.where(head_of_lane == hd, part, mixed)
            mixed_chunks.append(mixed + b_s_ref[l])
        mixed = jnp.concatenate(mixed_chunks, axis=0)
        y_d = (u * mixed) * _silu(d_g)

        y_cat = jnp.concatenate([y_a, y_b, y_c, y_d], axis=-1).astype(jnp.bfloat16)
        y = jnp.dot(y_cat, _unpack_bf16_rows(w_out_ref[l]), preferred_element_type=jnp.float32)
        return x_cur + gate * y

    x_cur = x_ref[...]
    for l in range(DEPTH):
        x_cur = layer(l, x_cur)
    o_ref[...] = _rms_norm(x_cur, final_g_ref[...])


def _const_spec(shape):
    zeros = (0,) * len(shape)
    return pl.BlockSpec(shape, lambda i: zeros, pipeline_mode=pl.Buffered(1))


def kernel(x, c, norm_g, w_ada, b_ada, w_in, w_conv_a, w_pool, pool_scale, w_dw_c, b_dw_c, ln_g_c, ln_b_c, w_pw2_c, b_pw2_c, ln_g_d, ln_b_d, w_s_d, b_s_d, w_out, final_g):
    f32 = jnp.float32
    assert x.shape == (1, SEQ, D_MODEL) and c.shape == (1, D_MODEL)
    assert POOL_WINDOWS == (2, 4, 8, 16) and N_CT == 2

    c8 = jnp.broadcast_to(c, (8, D_MODEL))
    n_col = 3
    mod = pl.pallas_call(
        _mod_kernel,
        grid=(DEPTH, n_col),
        in_specs=[pl.BlockSpec((8, D_MODEL), lambda l, j: (0, 0)),
                  pl.BlockSpec((1, D_MODEL, D_MODEL), lambda l, j: (l, 0, j)),
                  pl.BlockSpec((1, 1, D_MODEL), lambda l, j: (l, 0, j))],
        out_specs=pl.BlockSpec((1, 8, D_MODEL), lambda l, j: (l, 0, j)),
        out_shape=jax.ShapeDtypeStruct((DEPTH, 8, 3 * D_MODEL), f32),
        name="adaln_mod",
    )(c8, w_ada, b_ada.reshape(DEPTH, 1, 3 * D_MODEL))

    taps = jnp.zeros((DEPTH, N_TAPS, GW), f32)
    taps = taps.at[:, TAP_A0:TAP_A0 + SHORT_CONV].set(w_conv_a)
    taps = taps.at[:, TAP_C0:TAP_C0 + CONF_WIDTH].set(w_dw_c)
    vecs = jnp.stack([pool_scale, b_dw_c, ln_g_c, ln_b_c, b_pw2_c, ln_g_d, ln_b_d,
                      jnp.zeros_like(pool_scale)], axis=1)
    w_pool_bd = jnp.zeros((DEPTH, GW, GW), f32)
    for g in range(len(POOL_WINDOWS)):
        sl = slice(g * POOL_GROUP_DIM, (g + 1) * POOL_GROUP_DIM)
        w_pool_bd = w_pool_bd.at[:, sl, sl].set(w_pool[:, g])
    b_s_full = jnp.repeat(jnp.swapaxes(b_s_d, 1, 2), SGU_HEAD_DIM, axis=2)

    T = ROW_TILE
    out = pl.pallas_call(
        _trunk_kernel,
        grid=(SEQ // T,),
        in_specs=[
            pl.BlockSpec((T, D_MODEL), lambda i: (i, 0)),
            _const_spec((DEPTH, 8, 3 * D_MODEL)),
            _const_spec((DEPTH, D_MODEL)),
            _const_spec((1, D_MODEL)),
            _const_spec((DEPTH, D_MODEL // 2, D_IN)),
            _const_spec((DEPTH, D_MODEL // 2, D_MODEL)),
            _const_spec((DEPTH, N_TAPS, GW)),
            _const_spec((DEPTH, N_VEC, GW)),
            _const_spec((DEPTH, GW, GW)),
            _const_spec((DEPTH, GW, GW)),
            _const_spec((DEPTH, N_HEADS_D, SGU_CHUNK, SGU_CHUNK)),
            _const_spec((DEPTH, SGU_CHUNK, GW)),
        ],
        out_specs=pl.BlockSpec((T, D_MODEL), lambda i: (i, 0)),
        out_shape=jax.ShapeDtypeStruct((SEQ, D_MODEL), f32),
        scratch_shapes=[
            pltpu.VMEM((DEPTH, N_CT, HEAD_A + T, LANES), f32),
            pltpu.VMEM((DEPTH, N_CT, HEAD_P + T, LANES), f32),
            pltpu.VMEM((DEPTH, N_CT, HEAD_P + T, LANES), f32),
            pltpu.VMEM((DEPTH, HEAD_P + T, LANES), f32),
            pltpu.VMEM((DEPTH, HEAD_P + T, LANES), f32),
            pltpu.VMEM((DEPTH, N_CT, HEAD_C + T, LANES), f32),
        ],
        compiler_params=pltpu.CompilerParams(
            dimension_semantics=("arbitrary",),
            vmem_limit_bytes=VMEM_LIMIT_BYTES),
        name="hybrid_trunk",
    )(x.reshape(SEQ, D_MODEL), mod, norm_g, final_g.reshape(1, D_MODEL),
      _pack_bf16_rows(w_in, "pack_w_in"), _pack_bf16_rows(w_out, "pack_w_out"), taps, vecs,
      w_pool_bd.astype(jnp.bfloat16), w_pw2_c.astype(jnp.bfloat16), w_s_d, b_s_full)
    return out.reshape(1, SEQ, D_MODEL)
```

```python
import math

import jax
import jax.numpy as jnp
from jax import lax
from jax.experimental import pallas as pl
from jax.experimental.pallas import tpu as pltpu

D_MODEL = 1024
SEQ = 16384
DEPTH = 2
GW = 256
D_IN = 12 * GW
SHORT_CONV = 3
CONF_WIDTH = 31
POOL_WINDOWS = (2, 4, 8, 16)
POOL_GROUP_DIM = GW // len(POOL_WINDOWS)
SGU_CHUNK = 128
N_HEADS_D = 4
SGU_HEAD_DIM = GW // N_HEADS_D
EPS = 1e-6

LANES = 128
N_CT = GW // LANES
ROW_TILE = 512
HEAD_A = 8
HEAD_P = 8
HEAD_C = 32
VMEM_LIMIT_BYTES = 56 * 1024 * 1024
PACK_COLS = 512

V_POOL_SCALE, V_B_DW, V_LN_G_C, V_LN_B_C, V_B_PW2, V_LN_G_D, V_LN_B_D = range(7)
N_VEC = 8
N_TAPS = 40
TAP_A0 = 0
TAP_C0 = 8


def _silu(x):
    hx = 0.5 * x
    return hx + hx * jnp.tanh(hx)


def _sigmoid(x):
    return 0.5 + 0.5 * jnp.tanh(0.5 * x)


def _gelu_tanh(x):
    c = math.sqrt(2.0 / math.pi)
    hx = 0.5 * x
    return hx + hx * jnp.tanh(x * ((x * x) * (0.044715 * c) + c))


def _layer_norm(x, g, b):
    mu = jnp.mean(x, axis=-1, keepdims=True)
    xc = x - mu
    var = jnp.mean(xc * xc, axis=-1, keepdims=True)
    return xc * lax.rsqrt(var + EPS) * g + b


def _rms_norm(x, g):
    ms = jnp.mean(x * x, axis=-1, keepdims=True)
    return (x * lax.rsqrt(ms + EPS)) * g


def _pack_kernel(w_ref, o_ref):
    o_ref[0] = pltpu.bitcast(w_ref[0].astype(jnp.bfloat16), jnp.uint32)


def _pack_bf16_rows(w, name):
    depth, k, n = w.shape
    return pl.pallas_call(
        _pack_kernel,
        grid=(depth, n // PACK_COLS),
        in_specs=[pl.BlockSpec((1, k, PACK_COLS), lambda l, j: (l, 0, j))],
        out_specs=pl.BlockSpec((1, k // 2, PACK_COLS), lambda l, j: (l, 0, j)),
        out_shape=jax.ShapeDtypeStruct((depth, k // 2, n), jnp.uint32),
        name=name,
    )(w)


def _unpack_bf16_rows(w_u32):
    return pltpu.bitcast(w_u32, jnp.bfloat16)


def _mod_kernel(c_ref, w_ref, b_ref, o_ref):
    ca = _silu(c_ref[...]).astype(jnp.bfloat16)
    o_ref[0] = jnp.dot(ca, w_ref[0].astype(jnp.bfloat16),
                       preferred_element_type=jnp.float32) + b_ref[0]


def _trunk_kernel(x_ref, mod_ref, norm_g_ref, final_g_ref, w_in_ref, w_out_ref, taps_ref, vecs_ref,
                  w_pool_ref, w_pw2_ref, w_s_ref, b_s_ref, o_ref,
                  cx_buf, p_buf, s2_buf, s4_buf, s8_buf, h_buf):
    i = pl.program_id(0)
    T = ROW_TILE

    @pl.when(i == 0)
    def _():
        cx_buf[:, :, 0:HEAD_A, :] = jnp.zeros((DEPTH, N_CT, HEAD_A, LANES), jnp.float32)
        for buf in (p_buf, s2_buf):
            buf[:, :, 0:HEAD_P, :] = jnp.zeros((DEPTH, N_CT, HEAD_P, LANES), jnp.float32)
        for buf in (s4_buf, s8_buf):
            buf[:, 0:HEAD_P, :] = jnp.zeros((DEPTH, HEAD_P, LANES), jnp.float32)
        h_buf[:, :, 0:HEAD_C, :] = jnp.zeros((DEPTH, N_CT, HEAD_C, LANES), jnp.float32)

    lane = lax.broadcasted_iota(jnp.int32, (T, LANES), 1)
    left = lane < POOL_GROUP_DIM
    row1 = lax.broadcasted_iota(jnp.int32, (T, LANES), 0) + (i * T + 1)
    inv_count = [1.0 / jnp.minimum(row1, jnp.where(left, POOL_WINDOWS[2 * ct], POOL_WINDOWS[2 * ct + 1])
                                   ).astype(jnp.float32) for ct in range(N_CT)]
    head_of_lane = lax.broadcasted_iota(jnp.int32, (SGU_CHUNK, GW), 1) // SGU_HEAD_DIM
    tri = (lax.broadcasted_iota(jnp.int32, (SGU_CHUNK, SGU_CHUNK), 0)
           >= lax.broadcasted_iota(jnp.int32, (SGU_CHUNK, SGU_CHUNK), 1))

    def vec(l, k):
        return vecs_ref[l, k:k + 1, :]

    def shifted(buf, idx, head, back):
        return buf[idx + (slice(head - back, head - back + T), slice(None))]

    def carry(buf, idx, head):
        buf[idx + (slice(0, head), slice(None))] = buf[idx + (slice(T, T + head), slice(None))]

    def layer(l, x_cur):
        shift = mod_ref[l, 0:1, 0:D_MODEL]
        scale = mod_ref[l, 0:1, D_MODEL:2 * D_MODEL]
        gate = mod_ref[l, 0:1, 2 * D_MODEL:3 * D_MODEL]
        h = (_rms_norm(x_cur, norm_g_ref[l:l + 1, :] * (1.0 + scale)) + shift).astype(jnp.bfloat16)

        def short_conv(a_b, a_c, a_x, a_g):
            cx = a_c * a_x
            convs = []
            for ct in range(N_CT):
                lanes = slice(ct * LANES, (ct + 1) * LANES)
                cx_buf[l, ct, HEAD_A:HEAD_A + T, :] = cx[:, lanes]
                conv = None
                for j in range(SHORT_CONV):
                    term = (taps_ref[l, TAP_A0 + j:TAP_A0 + j + 1, lanes]
                            * shifted(cx_buf, (l, ct), HEAD_A, SHORT_CONV - 1 - j))
                    conv = term if conv is None else conv + term
                carry(cx_buf, (l, ct), HEAD_A)
                convs.append(conv)
            return (a_b * jnp.concatenate(convs, axis=-1)) * _silu(a_g)

        def pooling(b_p, b_g):
            pooled = []
            for ct in range(N_CT):
                p = b_p[:, ct * LANES:(ct + 1) * LANES]
                p_buf[l, ct, HEAD_P:HEAD_P + T, :] = p
                s2 = p + shifted(p_buf, (l, ct), HEAD_P, 1)
                s2_buf[l, ct, HEAD_P:HEAD_P + T, :] = s2
                s4 = s2 + shifted(s2_buf, (l, ct), HEAD_P, 2)
                carry(p_buf, (l, ct), HEAD_P)
                carry(s2_buf, (l, ct), HEAD_P)
                if ct == 0:
                    wsum = jnp.where(left, s2, s4)
                else:
                    s4_buf[l, HEAD_P:HEAD_P + T, :] = s4
                    s8 = s4 + shifted(s4_buf, (l,), HEAD_P, 4)
                    s8_buf[l, HEAD_P:HEAD_P + T, :] = s8
                    s16 = s8 + shifted(s8_buf, (l,), HEAD_P, 8)
                    carry(s4_buf, (l,), HEAD_P)
                    carry(s8_buf, (l,), HEAD_P)
                    wsum = jnp.where(left, s8, s16)
                pooled.append(wsum * inv_count[ct] - p)
            pooled = jnp.concatenate(pooled, axis=-1)
            y_b = jnp.dot(pooled.astype(jnp.bfloat16), w_pool_ref[l], preferred_element_type=jnp.float32)
            return (y_b * vec(l, V_POOL_SCALE)) * _silu(b_g)

        def conformer(c_a, c_gl, c_g):
            hgl = c_a * _sigmoid(c_gl)
            accs = []
            for ct in range(N_CT):
                lanes = slice(ct * LANES, (ct + 1) * LANES)
                h_buf[l, ct, HEAD_C:HEAD_C + T, :] = hgl[:, lanes]
                acc = None
                for j in range(CONF_WIDTH):
                    term = (taps_ref[l, TAP_C0 + j:TAP_C0 + j + 1, lanes]
                            * shifted(h_buf, (l, ct), HEAD_C, CONF_WIDTH - 1 - j))
                    acc = term if acc is None else acc + term
                carry(h_buf, (l, ct), HEAD_C)
                accs.append(acc)
            hc = jnp.concatenate(accs, axis=-1) + vec(l, V_B_DW)
            hc = _silu(_layer_norm(hc, vec(l, V_LN_G_C), vec(l, V_LN_B_C)))
            y_c = jnp.dot(hc.astype(jnp.bfloat16), w_pw2_ref[l], preferred_element_type=jnp.float32)
            return (y_c + vec(l, V_B_PW2)) * _silu(c_g)

        def sgu(d_u, d_v, d_g):
            u = _gelu_tanh(d_u)
            v = _layer_norm(_gelu_tanh(d_v), vec(l, V_LN_G_D), vec(l, V_LN_B_D))
            ws = jnp.concatenate([jnp.where(tri, w_s_ref[l, hd], 0.0).astype(jnp.bfloat16)
                                  for hd in range(N_HEADS_D)], axis=1)
            mixed_chunks = []
            for ck in range(T // SGU_CHUNK):
                v_ck = v[ck * SGU_CHUNK:(ck + 1) * SGU_CHUNK, :]
                stacked = jnp.concatenate([jnp.where(head_of_lane == hd, v_ck, 0.0).astype(jnp.bfloat16)
                                           for hd in range(N_HEADS_D)], axis=0)
                mixed_chunks.append(jnp.dot(ws, stacked, preferred_element_type=jnp.float32) + b_s_ref[l])
            mixed = jnp.concatenate(mixed_chunks, axis=0)
            return (u * mixed) * _silu(d_g)

        z = jnp.dot(h, _unpack_bf16_rows(w_in_ref[l]), preferred_element_type=jnp.float32)
        zs = [z[:, k * GW:(k + 1) * GW] for k in range(D_IN // GW)]
        y_cat = jnp.concatenate([short_conv(*zs[0:4]), pooling(*zs[4:6]), conformer(*zs[6:9]),
                                 sgu(*zs[9:12])], axis=-1).astype(jnp.bfloat16)
        y = jnp.dot(y_cat, _unpack_bf16_rows(w_out_ref[l]), preferred_element_type=jnp.float32)
        return x_cur + gate * y

    x_cur = x_ref[...]
    for l in range(DEPTH):
        x_cur = layer(l, x_cur)
    o_ref[...] = _rms_norm(x_cur, final_g_ref[...])


def _const_spec(shape):
    zeros = (0,) * len(shape)
    return pl.BlockSpec(shape, lambda i: zeros, pipeline_mode=pl.Buffered(1))


def kernel(x, c, norm_g, w_ada, b_ada, w_in, w_conv_a, w_pool, pool_scale, w_dw_c, b_dw_c, ln_g_c, ln_b_c, w_pw2_c, b_pw2_c, ln_g_d, ln_b_d, w_s_d, b_s_d, w_out, final_g):
    f32 = jnp.float32
    assert x.shape == (1, SEQ, D_MODEL) and c.shape == (1, D_MODEL)
    assert POOL_WINDOWS == (2, 4, 8, 16) and N_CT == 2

    c8 = jnp.broadcast_to(c, (8, D_MODEL))
    n_col = 3
    mod = pl.pallas_call(
        _mod_kernel,
        grid=(DEPTH, n_col),
        in_specs=[pl.BlockSpec((8, D_MODEL), lambda l, j: (0, 0)),
                  pl.BlockSpec((1, D_MODEL, D_MODEL), lambda l, j: (l, 0, j)),
                  pl.BlockSpec((1, 1, D_MODEL), lambda l, j: (l, 0, j))],
        out_specs=pl.BlockSpec((1, 8, D_MODEL), lambda l, j: (l, 0, j)),
        out_shape=jax.ShapeDtypeStruct((DEPTH, 8, 3 * D_MODEL), f32),
        name="adaln_mod",
    )(c8, w_ada, b_ada.reshape(DEPTH, 1, 3 * D_MODEL))

    taps = jnp.zeros((DEPTH, N_TAPS, GW), f32)
    taps = taps.at[:, TAP_A0:TAP_A0 + SHORT_CONV].set(w_conv_a)
    taps = taps.at[:, TAP_C0:TAP_C0 + CONF_WIDTH].set(w_dw_c)
    vecs = jnp.stack([pool_scale, b_dw_c, ln_g_c, ln_b_c, b_pw2_c, ln_g_d, ln_b_d,
                      jnp.zeros_like(pool_scale)], axis=1)
    w_pool_bd = jnp.zeros((DEPTH, GW, GW), f32)
    for g in range(len(POOL_WINDOWS)):
        sl = slice(g * POOL_GROUP_DIM, (g + 1) * POOL_GROUP_DIM)
        w_pool_bd = w_pool_bd.at[:, sl, sl].set(w_pool[:, g])
    b_s_full = jnp.repeat(jnp.swapaxes(b_s_d, 1, 2), SGU_HEAD_DIM, axis=2)

    T = ROW_TILE
    out = pl.pallas_call(
        _trunk_kernel,
        grid=(SEQ // T,),
        in_specs=[
            pl.BlockSpec((T, D_MODEL), lambda i: (i, 0)),
            _const_spec((DEPTH, 8, 3 * D_MODEL)),
            _const_spec((DEPTH, D_MODEL)),
            _const_spec((1, D_MODEL)),
            _const_spec((DEPTH, D_MODEL // 2, D_IN)),
            _const_spec((DEPTH, D_MODEL // 2, D_MODEL)),
            _const_spec((DEPTH, N_TAPS, GW)),
            _const_spec((DEPTH, N_VEC, GW)),
            _const_spec((DEPTH, GW, GW)),
            _const_spec((DEPTH, GW, GW)),
            _const_spec((DEPTH, N_HEADS_D, SGU_CHUNK, SGU_CHUNK)),
            _const_spec((DEPTH, SGU_CHUNK, GW)),
        ],
        out_specs=pl.BlockSpec((T, D_MODEL), lambda i: (i, 0)),
        out_shape=jax.ShapeDtypeStruct((SEQ, D_MODEL), f32),
        scratch_shapes=[
            pltpu.VMEM((DEPTH, N_CT, HEAD_A + T, LANES), f32),
            pltpu.VMEM((DEPTH, N_CT, HEAD_P + T, LANES), f32),
            pltpu.VMEM((DEPTH, N_CT, HEAD_P + T, LANES), f32),
            pltpu.VMEM((DEPTH, HEAD_P + T, LANES), f32),
            pltpu.VMEM((DEPTH, HEAD_P + T, LANES), f32),
            pltpu.VMEM((DEPTH, N_CT, HEAD_C + T, LANES), f32),
        ],
        compiler_params=pltpu.CompilerParams(
            dimension_semantics=("arbitrary",),
            vmem_limit_bytes=VMEM_LIMIT_BYTES),
        name="hybrid_trunk",
    )(x.reshape(SEQ, D_MODEL), mod, norm_g, final_g.reshape(1, D_MODEL),
      _pack_bf16_rows(w_in, "pack_w_in"), _pack_bf16_rows(w_out, "pack_w_out"), taps, vecs,
      w_pool_bd.astype(jnp.bfloat16), w_pw2_c.astype(jnp.bfloat16), w_s_d, b_s_full)
    return out.reshape(1, SEQ, D_MODEL)
```

```python
import math

import jax
import jax.numpy as jnp
from jax import lax
from jax.experimental import pallas as pl
from jax.experimental.pallas import tpu as pltpu

D_MODEL = 1024
SEQ = 16384
DEPTH = 2
GW = 256
D_IN = 12 * GW
SHORT_CONV = 3
CONF_WIDTH = 31
POOL_WINDOWS = (2, 4, 8, 16)
POOL_GROUP_DIM = GW // len(POOL_WINDOWS)
SGU_CHUNK = 128
N_HEADS_D = 4
SGU_HEAD_DIM = GW // N_HEADS_D
EPS = 1e-6

LANES = 128
N_CT = GW // LANES
ROW_TILE = 512
HEAD_A = 8
HEAD_P = 8
HEAD_C = 32
VMEM_LIMIT_BYTES = 56 * 1024 * 1024
PACK_COLS = 512


def _silu(x):
    hx = 0.5 * x
    return hx + hx * jnp.tanh(hx)


def _sigmoid(x):
    return 0.5 + 0.5 * jnp.tanh(0.5 * x)


def _gelu_tanh(x):
    c = math.sqrt(2.0 / math.pi)
    hx = 0.5 * x
    return hx + hx * jnp.tanh(x * ((x * x) * (0.044715 * c) + c))


def _layer_norm(x, g, b):
    mu = jnp.mean(x, axis=-1, keepdims=True)
    xc = x - mu
    var = jnp.mean(xc * xc, axis=-1, keepdims=True)
    return xc * lax.rsqrt(var + EPS) * g + b


def _rms_norm(x, g):
    ms = jnp.mean(x * x, axis=-1, keepdims=True)
    return (x * lax.rsqrt(ms + EPS)) * g


def _pack_kernel(w_ref, o_ref):
    o_ref[0] = pltpu.bitcast(w_ref[0].astype(jnp.bfloat16), jnp.uint32)


def _pack_bf16_rows(w, name):
    depth, k, n = w.shape
    return pl.pallas_call(
        _pack_kernel,
        grid=(depth, n // PACK_COLS),
        in_specs=[pl.BlockSpec((1, k, PACK_COLS), lambda l, j: (l, 0, j))],
        out_specs=pl.BlockSpec((1, k // 2, PACK_COLS), lambda l, j: (l, 0, j)),
        out_shape=jax.ShapeDtypeStruct((depth, k // 2, n), jnp.uint32),
        name=name,
    )(w)


def _unpack_bf16_rows(w_u32):
    return pltpu.bitcast(w_u32, jnp.bfloat16)


def _mod_kernel(c_ref, w_ref, b_ref, o_ref):
    ca = _silu(jnp.broadcast_to(c_ref[...], (8, D_MODEL))).astype(jnp.bfloat16)
    b = b_ref[pl.ds(pl.program_id(0), 1), :]
    o_ref[0] = jnp.dot(ca, w_ref[0].astype(jnp.bfloat16), preferred_element_type=jnp.float32) + b


def _trunk_kernel(x_ref, mod_ref, norm_g_ref, final_g_ref, w_in_ref, w_out_ref, w_conv_a_ref, w_dw_c_ref,
                  pool_scale_ref, b_dw_c_ref, ln_g_c_ref, ln_b_c_ref, b_pw2_c_ref, ln_g_d_ref, ln_b_d_ref,
                  w_pool_ref, w_pw2_ref, w_s_ref, b_s_ref, o_ref,
                  cx_buf, p_buf, s2_buf, s4_buf, s8_buf, h_buf):
    i = pl.program_id(0)
    T = ROW_TILE

    @pl.when(i == 0)
    def _():
        cx_buf[:, :, 0:HEAD_A, :] = jnp.zeros((DEPTH, N_CT, HEAD_A, LANES), jnp.float32)
        for buf in (p_buf, s2_buf):
            buf[:, :, 0:HEAD_P, :] = jnp.zeros((DEPTH, N_CT, HEAD_P, LANES), jnp.float32)
        for buf in (s4_buf, s8_buf):
            buf[:, 0:HEAD_P, :] = jnp.zeros((DEPTH, HEAD_P, LANES), jnp.float32)
        h_buf[:, :, 0:HEAD_C, :] = jnp.zeros((DEPTH, N_CT, HEAD_C, LANES), jnp.float32)

    lane = lax.broadcasted_iota(jnp.int32, (T, LANES), 1)
    left = lane < POOL_GROUP_DIM
    row1 = lax.broadcasted_iota(jnp.int32, (T, LANES), 0) + (i * T + 1)
    inv_count = [1.0 / jnp.minimum(row1, jnp.where(left, POOL_WINDOWS[2 * ct], POOL_WINDOWS[2 * ct + 1])
                                   ).astype(jnp.float32) for ct in range(N_CT)]
    head_of_lane = lax.broadcasted_iota(jnp.int32, (SGU_CHUNK, GW), 1) // SGU_HEAD_DIM
    tri = (lax.broadcasted_iota(jnp.int32, (SGU_CHUNK, SGU_CHUNK), 0)
           >= lax.broadcasted_iota(jnp.int32, (SGU_CHUNK, SGU_CHUNK), 1))

    def vec(l, ref):
        return ref[l:l + 1, :]

    def shifted(buf, idx, head, back):
        return buf[idx + (slice(head - back, head - back + T), slice(None))]

    def carry(buf, idx, head):
        buf[idx + (slice(0, head), slice(None))] = buf[idx + (slice(T, T + head), slice(None))]

    def layer(l, x_cur):
        shift = mod_ref[l, 0:1, 0:D_MODEL]
        scale = mod_ref[l, 0:1, D_MODEL:2 * D_MODEL]
        gate = mod_ref[l, 0:1, 2 * D_MODEL:3 * D_MODEL]
        h = (_rms_norm(x_cur, norm_g_ref[l:l + 1, :] * (1.0 + scale)) + shift).astype(jnp.bfloat16)

        def short_conv(a_b, a_c, a_x, a_g):
            cx = a_c * a_x
            convs = []
            for ct in range(N_CT):
                lanes = slice(ct * LANES, (ct + 1) * LANES)
                cx_buf[l, ct, HEAD_A:HEAD_A + T, :] = cx[:, lanes]
                conv = None
                for j in range(SHORT_CONV):
                    term = (w_conv_a_ref[l, j:j + 1, lanes]
                            * shifted(cx_buf, (l, ct), HEAD_A, SHORT_CONV - 1 - j))
                    conv = term if conv is None else conv + term
                carry(cx_buf, (l, ct), HEAD_A)
                convs.append(conv)
            return (a_b * jnp.concatenate(convs, axis=-1)) * _silu(a_g)

        def pooling(b_p, b_g):
            pooled = []
            for ct in range(N_CT):
                p = b_p[:, ct * LANES:(ct + 1) * LANES]
                p_buf[l, ct, HEAD_P:HEAD_P + T, :] = p
                s2 = p + shifted(p_buf, (l, ct), HEAD_P, 1)
                s2_buf[l, ct, HEAD_P:HEAD_P + T, :] = s2
                s4 = s2 + shifted(s2_buf, (l, ct), HEAD_P, 2)
                carry(p_buf, (l, ct), HEAD_P)
                carry(s2_buf, (l, ct), HEAD_P)
                if ct == 0:
                    wsum = jnp.where(left, s2, s4)
                else:
                    s4_buf[l, HEAD_P:HEAD_P + T, :] = s4
                    s8 = s4 + shifted(s4_buf, (l,), HEAD_P, 4)
                    s8_buf[l, HEAD_P:HEAD_P + T, :] = s8
                    s16 = s8 + shifted(s8_buf, (l,), HEAD_P, 8)
                    carry(s4_buf, (l,), HEAD_P)
                    carry(s8_buf, (l,), HEAD_P)
                    wsum = jnp.where(left, s8, s16)
                pooled.append(wsum * inv_count[ct] - p)
            pooled = jnp.concatenate(pooled, axis=-1)
            y_b = jnp.dot(pooled.astype(jnp.bfloat16), w_pool_ref[l], preferred_element_type=jnp.float32)
            return (y_b * vec(l, pool_scale_ref)) * _silu(b_g)

        def conformer(c_a, c_gl, c_g):
            hgl = c_a * _sigmoid(c_gl)
            accs = []
            for ct in range(N_CT):
                lanes = slice(ct * LANES, (ct + 1) * LANES)
                h_buf[l, ct, HEAD_C:HEAD_C + T, :] = hgl[:, lanes]
                acc = None
                for j in range(CONF_WIDTH):
                    term = (w_dw_c_ref[l, j:j + 1, lanes]
                            * shifted(h_buf, (l, ct), HEAD_C, CONF_WIDTH - 1 - j))
                    acc = term if acc is None else acc + term
                carry(h_buf, (l, ct), HEAD_C)
                accs.append(acc)
            hc = jnp.concatenate(accs, axis=-1) + vec(l, b_dw_c_ref)
            hc = _silu(_layer_norm(hc, vec(l, ln_g_c_ref), vec(l, ln_b_c_ref)))
            y_c = jnp.dot(hc.astype(jnp.bfloat16), w_pw2_ref[l], preferred_element_type=jnp.float32)
            return (y_c + vec(l, b_pw2_c_ref)) * _silu(c_g)

        def sgu(d_u, d_v, d_g):
            u = _gelu_tanh(d_u)
            v = _layer_norm(_gelu_tanh(d_v), vec(l, ln_g_d_ref), vec(l, ln_b_d_ref))
            ws = jnp.concatenate([jnp.where(tri, w_s_ref[l, hd], 0.0).astype(jnp.bfloat16)
                                  for hd in range(N_HEADS_D)], axis=1)
            mixed_chunks = []
            for ck in range(T // SGU_CHUNK):
                v_ck = v[ck * SGU_CHUNK:(ck + 1) * SGU_CHUNK, :]
                stacked = jnp.concatenate([jnp.where(head_of_lane == hd, v_ck, 0.0).astype(jnp.bfloat16)
                                           for hd in range(N_HEADS_D)], axis=0)
                mixed_chunks.append(jnp.dot(ws, stacked, preferred_element_type=jnp.float32) + b_s_ref[l])
            mixed = jnp.concatenate(mixed_chunks, axis=0)
            return (u * mixed) * _silu(d_g)

        z = jnp.dot(h, _unpack_bf16_rows(w_in_ref[l]), preferred_element_type=jnp.float32)
        zs = [z[:, k * GW:(k + 1) * GW] for k in range(D_IN // GW)]
        y_cat = jnp.concatenate([short_conv(*zs[0:4]), pooling(*zs[4:6]), conformer(*zs[6:9]),
                                 sgu(*zs[9:12])], axis=-1).astype(jnp.bfloat16)
        y = jnp.dot(y_cat, _unpack_bf16_rows(w_out_ref[l]), preferred_element_type=jnp.float32)
        return x_cur + gate * y

    x_cur = x_ref[...]
    for l in range(DEPTH):
        x_cur = layer(l, x_cur)
    o_ref[...] = _rms_norm(x_cur, final_g_ref[...])


def _const_spec(shape):
    zeros = (0,) * len(shape)
    return pl.BlockSpec(shape, lambda i: zeros, pipeline_mode=pl.Buffered(1))


def kernel(x, c, norm_g, w_ada, b_ada, w_in, w_conv_a, w_pool, pool_scale, w_dw_c, b_dw_c, ln_g_c, ln_b_c, w_pw2_c, b_pw2_c, ln_g_d, ln_b_d, w_s_d, b_s_d, w_out, final_g):
    f32 = jnp.float32
    assert x.shape == (1, SEQ, D_MODEL) and c.shape == (1, D_MODEL)
    assert POOL_WINDOWS == (2, 4, 8, 16) and N_CT == 2

    n_col = 3
    mod = pl.pallas_call(
        _mod_kernel,
        grid=(DEPTH, n_col),
        in_specs=[pl.BlockSpec((1, D_MODEL), lambda l, j: (0, 0)),
                  pl.BlockSpec((1, D_MODEL, D_MODEL), lambda l, j: (l, 0, j)),
                  pl.BlockSpec((DEPTH, D_MODEL), lambda l, j: (0, j))],
        out_specs=pl.BlockSpec((1, 8, D_MODEL), lambda l, j: (l, 0, j)),
        out_shape=jax.ShapeDtypeStruct((DEPTH, 8, 3 * D_MODEL), f32),
        name="adaln_mod",
    )(c, w_ada, b_ada)

    n_grp = len(POOL_WINDOWS)
    w_pool_bd = (w_pool[:, :, :, None, :] * jnp.eye(n_grp, dtype=f32)[None, :, None, :, None]
                 ).reshape(DEPTH, GW, GW).astype(jnp.bfloat16)
    b_s_full = jnp.broadcast_to(jnp.swapaxes(b_s_d, 1, 2)[:, :, :, None],
                                (DEPTH, SGU_CHUNK, N_HEADS_D, SGU_HEAD_DIM)).reshape(DEPTH, SGU_CHUNK, GW)
    vec_specs = [_const_spec((DEPTH, GW))] * 7

    T = ROW_TILE
    out = pl.pallas_call(
        _trunk_kernel,
        grid=(SEQ // T,),
        in_specs=[
            pl.BlockSpec((T, D_MODEL), lambda i: (i, 0)),
            _const_spec((DEPTH, 8, 3 * D_MODEL)),
            _const_spec((DEPTH, D_MODEL)),
            _const_spec((1, D_MODEL)),
            _const_spec((DEPTH, D_MODEL // 2, D_IN)),
            _const_spec((DEPTH, D_MODEL // 2, D_MODEL)),
            _const_spec((DEPTH, SHORT_CONV, GW)),
            _const_spec((DEPTH, CONF_WIDTH, GW)),
            *vec_specs,
            _const_spec((DEPTH, GW, GW)),
            _const_spec((DEPTH, GW, GW)),
            _const_spec((DEPTH, N_HEADS_D, SGU_CHUNK, SGU_CHUNK)),
            _const_spec((DEPTH, SGU_CHUNK, GW)),
        ],
        out_specs=pl.BlockSpec((T, D_MODEL), lambda i: (i, 0)),
        out_shape=jax.ShapeDtypeStruct((SEQ, D_MODEL), f32),
        scratch_shapes=[
            pltpu.VMEM((DEPTH, N_CT, HEAD_A + T, LANES), f32),
            pltpu.VMEM((DEPTH, N_CT, HEAD_P + T, LANES), f32),
            pltpu.VMEM((DEPTH, N_CT, HEAD_P + T, LANES), f32),
            pltpu.VMEM((DEPTH, HEAD_P + T, LANES), f32),
            pltpu.VMEM((DEPTH, HEAD_P + T, LANES), f32),
            pltpu.VMEM((DEPTH, N_CT, HEAD_C + T, LANES), f32),
        ],
        compiler_params=pltpu.CompilerParams(
            dimension_semantics=("arbitrary",),
            vmem_limit_bytes=VMEM_LIMIT_BYTES),
        name="hybrid_trunk",
    )(x.reshape(SEQ, D_MODEL), mod, norm_g, final_g.reshape(1, D_MODEL),
      _pack_bf16_rows(w_in, "pack_w_in"), _pack_bf16_rows(w_out, "pack_w_out"), w_conv_a, w_dw_c,
      pool_scale, b_dw_c, ln_g_c, ln_b_c, b_pw2_c, ln_g_d, ln_b_d,
      w_pool_bd, w_pw2_c.astype(jnp.bfloat16), w_s_d, b_s_full)
    return out.reshape(1, SEQ, D_MODEL)
```

```python
import math

import jax
import jax.numpy as jnp
from jax import lax
from jax.experimental import pallas as pl
from jax.experimental.pallas import tpu as pltpu

D_MODEL = 1024
SEQ = 16384
DEPTH = 2
GW = 256
D_IN = 12 * GW
SHORT_CONV = 3
CONF_WIDTH = 31
POOL_WINDOWS = (2, 4, 8, 16)
POOL_GROUP_DIM = GW // len(POOL_WINDOWS)
SGU_CHUNK = 128
N_HEADS_D = 4
SGU_HEAD_DIM = GW // N_HEADS_D
EPS = 1e-6

LANES = 128
N_CT = GW // LANES
ROW_TILE = 512
HEAD_A = 8
HEAD_P = 8
HEAD_C = 32
VMEM_LIMIT_BYTES = 56 * 1024 * 1024
PREP_CHUNKS = 8
N_PREP = DEPTH * PREP_CHUNKS
PREP_IN_COLS = D_IN // PREP_CHUNKS
PREP_OUT_COLS = D_MODEL // PREP_CHUNKS


def _silu(x):
    hx = 0.5 * x
    return hx + hx * jnp.tanh(hx)


def _sigmoid(x):
    return 0.5 + 0.5 * jnp.tanh(0.5 * x)


def _gelu_tanh(x):
    c = math.sqrt(2.0 / math.pi)
    hx = 0.5 * x
    return hx + hx * jnp.tanh(x * ((x * x) * (0.044715 * c) + c))


def _layer_norm(x, g, b):
    mu = jnp.mean(x, axis=-1, keepdims=True)
    xc = x - mu
    var = jnp.mean(xc * xc, axis=-1, keepdims=True)
    return xc * lax.rsqrt(var + EPS) * g + b


def _rms_norm(x, g):
    ms = jnp.mean(x * x, axis=-1, keepdims=True)
    return (x * lax.rsqrt(ms + EPS)) * g


def _trunk_kernel(x_ref, c_ref, w_ada_ref, b_ada_ref, norm_g_ref, final_g_ref, w_in_ref, w_out_ref,
                  w_conv_a_ref, w_dw_c_ref, pool_scale_ref, b_dw_c_ref, ln_g_c_ref, ln_b_c_ref, b_pw2_c_ref,
                  ln_g_d_ref, ln_b_d_ref, w_pool_ref, w_pw2_ref, w_s_ref, b_s_ref, o_ref,
                  cx_buf, p_buf, s2_buf, s4_buf, s8_buf, h_buf, mod_buf, w_in_buf, w_out_buf):
    g = pl.program_id(0)
    T = ROW_TILE

    for s in range(N_PREP):
        l, j = divmod(s, PREP_CHUNKS)

        @pl.when(g == s)
        def _(l=l, j=j):
            in_cols = slice(j * PREP_IN_COLS, (j + 1) * PREP_IN_COLS)
            out_cols = slice(j * PREP_OUT_COLS, (j + 1) * PREP_OUT_COLS)
            w_in_buf[l, :, in_cols] = w_in_ref[0].astype(jnp.bfloat16)
            w_out_buf[l, :, out_cols] = w_out_ref[0].astype(jnp.bfloat16)
            ca = _silu(jnp.broadcast_to(c_ref[...], (8, D_MODEL))).astype(jnp.bfloat16)
            mod_buf[l, :, in_cols] = (jnp.dot(ca, w_ada_ref[0].astype(jnp.bfloat16),
                                              preferred_element_type=jnp.float32)
                                      + b_ada_ref[l:l + 1, in_cols])

    @pl.when(g == 0)
    def _():
        cx_buf[:, :, 0:HEAD_A, :] = jnp.zeros((DEPTH, N_CT, HEAD_A, LANES), jnp.float32)
        for buf in (p_buf, s2_buf):
            buf[:, :, 0:HEAD_P, :] = jnp.zeros((DEPTH, N_CT, HEAD_P, LANES), jnp.float32)
        for buf in (s4_buf, s8_buf):
            buf[:, 0:HEAD_P, :] = jnp.zeros((DEPTH, HEAD_P, LANES), jnp.float32)
        h_buf[:, :, 0:HEAD_C, :] = jnp.zeros((DEPTH, N_CT, HEAD_C, LANES), jnp.float32)

    @pl.when(g >= N_PREP)
    def _():
        _trunk_step(g - N_PREP, x_ref, mod_buf, norm_g_ref, final_g_ref, w_in_buf, w_out_buf,
                    w_conv_a_ref, w_dw_c_ref, pool_scale_ref, b_dw_c_ref, ln_g_c_ref, ln_b_c_ref, b_pw2_c_ref,
                    ln_g_d_ref, ln_b_d_ref, w_pool_ref, w_pw2_ref, w_s_ref, b_s_ref, o_ref,
                    cx_buf, p_buf, s2_buf, s4_buf, s8_buf, h_buf)


def _trunk_step(i, x_ref, mod_ref, norm_g_ref, final_g_ref, w_in_ref, w_out_ref,
                w_conv_a_ref, w_dw_c_ref, pool_scale_ref, b_dw_c_ref, ln_g_c_ref, ln_b_c_ref, b_pw2_c_ref,
                ln_g_d_ref, ln_b_d_ref, w_pool_ref, w_pw2_ref, w_s_ref, b_s_ref, o_ref,
                cx_buf, p_buf, s2_buf, s4_buf, s8_buf, h_buf):
    T = ROW_TILE
    lane = lax.broadcasted_iota(jnp.int32, (T, LANES), 1)
    left = lane < POOL_GROUP_DIM
    row1 = lax.broadcasted_iota(jnp.int32, (T, LANES), 0) + (i * T + 1)
    inv_count = [1.0 / jnp.minimum(row1, jnp.where(left, POOL_WINDOWS[2 * ct], POOL_WINDOWS[2 * ct + 1])
                                   ).astype(jnp.float32) for ct in range(N_CT)]
    head_of_lane = lax.broadcasted_iota(jnp.int32, (SGU_CHUNK, GW), 1) // SGU_HEAD_DIM
    tri = (lax.broadcasted_iota(jnp.int32, (SGU_CHUNK, SGU_CHUNK), 0)
           >= lax.broadcasted_iota(jnp.int32, (SGU_CHUNK, SGU_CHUNK), 1))

    def vec(l, ref):
        return ref[l:l + 1, :]

    def shifted(buf, idx, head, back):
        return buf[idx + (slice(head - back, head - back + T), slice(None))]

    def carry(buf, idx, head):
        buf[idx + (slice(0, head), slice(None))] = buf[idx + (slice(T, T + head), slice(None))]

    def layer(l, x_cur):
        shift = mod_ref[l, 0:1, 0:D_MODEL]
        scale = mod_ref[l, 0:1, D_MODEL:2 * D_MODEL]
        gate = mod_ref[l, 0:1, 2 * D_MODEL:3 * D_MODEL]
        h = (_rms_norm(x_cur, norm_g_ref[l:l + 1, :] * (1.0 + scale)) + shift).astype(jnp.bfloat16)

        def short_conv(a_b, a_c, a_x, a_g):
            cx = a_c * a_x
            convs = []
            for ct in range(N_CT):
                lanes = slice(ct * LANES, (ct + 1) * LANES)
                cx_buf[l, ct, HEAD_A:HEAD_A + T, :] = cx[:, lanes]
                conv = None
                for j in range(SHORT_CONV):
                    term = (w_conv_a_ref[l, j:j + 1, lanes]
                            * shifted(cx_buf, (l, ct), HEAD_A, SHORT_CONV - 1 - j))
                    conv = term if conv is None else conv + term
                carry(cx_buf, (l, ct), HEAD_A)
                convs.append(conv)
            return (a_b * jnp.concatenate(convs, axis=-1)) * _silu(a_g)

        def pooling(b_p, b_g):
            pooled = []
            for ct in range(N_CT):
                p = b_p[:, ct * LANES:(ct + 1) * LANES]
                p_buf[l, ct, HEAD_P:HEAD_P + T, :] = p
                s2 = p + shifted(p_buf, (l, ct), HEAD_P, 1)
                s2_buf[l, ct, HEAD_P:HEAD_P + T, :] = s2
                s4 = s2 + shifted(s2_buf, (l, ct), HEAD_P, 2)
                carry(p_buf, (l, ct), HEAD_P)
                carry(s2_buf, (l, ct), HEAD_P)
                if ct == 0:
                    wsum = jnp.where(left, s2, s4)
                else:
                    s4_buf[l, HEAD_P:HEAD_P + T, :] = s4
                    s8 = s4 + shifted(s4_buf, (l,), HEAD_P, 4)
                    s8_buf[l, HEAD_P:HEAD_P + T, :] = s8
                    s16 = s8 + shifted(s8_buf, (l,), HEAD_P, 8)
                    carry(s4_buf, (l,), HEAD_P)
                    carry(s8_buf, (l,), HEAD_P)
                    wsum = jnp.where(left, s8, s16)
                pooled.append(wsum * inv_count[ct] - p)
            pooled = jnp.concatenate(pooled, axis=-1)
            y_b = jnp.dot(pooled.astype(jnp.bfloat16), w_pool_ref[l], preferred_element_type=jnp.float32)
            return (y_b * vec(l, pool_scale_ref)) * _silu(b_g)

        def conformer(c_a, c_gl, c_g):
            hgl = c_a * _sigmoid(c_gl)
            accs = []
            for ct in range(N_CT):
                lanes = slice(ct * LANES, (ct + 1) * LANES)
                h_buf[l, ct, HEAD_C:HEAD_C + T, :] = hgl[:, lanes]
                acc = None
                for j in range(CONF_WIDTH):
                    term = (w_dw_c_ref[l, j:j + 1, lanes]
                            * shifted(h_buf, (l, ct), HEAD_C, CONF_WIDTH - 1 - j))
                    acc = term if acc is None else acc + term
                carry(h_buf, (l, ct), HEAD_C)
                accs.append(acc)
            hc = jnp.concatenate(accs, axis=-1) + vec(l, b_dw_c_ref)
            hc = _silu(_layer_norm(hc, vec(l, ln_g_c_ref), vec(l, ln_b_c_ref)))
            y_c = jnp.dot(hc.astype(jnp.bfloat16), w_pw2_ref[l], preferred_element_type=jnp.float32)
            return (y_c + vec(l, b_pw2_c_ref)) * _silu(c_g)

        def sgu(d_u, d_v, d_g):
            u = _gelu_tanh(d_u)
            v = _layer_norm(_gelu_tanh(d_v), vec(l, ln_g_d_ref), vec(l, ln_b_d_ref))
            ws = jnp.concatenate([jnp.where(tri, w_s_ref[l, hd], 0.0).astype(jnp.bfloat16)
                                  for hd in range(N_HEADS_D)], axis=1)
            mixed_chunks = []
            for ck in range(T // SGU_CHUNK):
                v_ck = v[ck * SGU_CHUNK:(ck + 1) * SGU_CHUNK, :]
                stacked = jnp.concatenate([jnp.where(head_of_lane == hd, v_ck, 0.0).astype(jnp.bfloat16)
                                           for hd in range(N_HEADS_D)], axis=0)
                mixed_chunks.append(jnp.dot(ws, stacked, preferred_element_type=jnp.float32) + b_s_ref[l])
            mixed = jnp.concatenate(mixed_chunks, axis=0)
            return (u * mixed) * _silu(d_g)

        z = jnp.dot(h, w_in_ref[l], preferred_element_type=jnp.float32)
        zs = [z[:, k * GW:(k + 1) * GW] for k in range(D_IN // GW)]
        y_cat = jnp.concatenate([short_conv(*zs[0:4]), pooling(*zs[4:6]), conformer(*zs[6:9]),
                                 sgu(*zs[9:12])], axis=-1).astype(jnp.bfloat16)
        y = jnp.dot(y_cat, w_out_ref[l], preferred_element_type=jnp.float32)
        return x_cur + gate * y

    x_cur = x_ref[...]
    for l in range(DEPTH):
        x_cur = layer(l, x_cur)
    o_ref[...] = _rms_norm(x_cur, final_g_ref[...])


def _const_spec(shape):
    zeros = (0,) * len(shape)
    return pl.BlockSpec(shape, lambda i: zeros, pipeline_mode=pl.Buffered(1))


def kernel(x, c, norm_g, w_ada, b_ada, w_in, w_conv_a, w_pool, pool_scale, w_dw_c, b_dw_c, ln_g_c, ln_b_c, w_pw2_c, b_pw2_c, ln_g_d, ln_b_d, w_s_d, b_s_d, w_out, final_g):
    f32 = jnp.float32
    assert x.shape == (1, SEQ, D_MODEL) and c.shape == (1, D_MODEL)
    assert POOL_WINDOWS == (2, 4, 8, 16) and N_CT == 2

    n_grp = len(POOL_WINDOWS)
    w_pool_bd = (w_pool[:, :, :, None, :] * jnp.eye(n_grp, dtype=f32)[None, :, None, :, None]
                 ).reshape(DEPTH, GW, GW).astype(jnp.bfloat16)
    b_s_full = jnp.broadcast_to(jnp.swapaxes(b_s_d, 1, 2)[:, :, :, None],
                                (DEPTH, SGU_CHUNK, N_HEADS_D, SGU_HEAD_DIM)).reshape(DEPTH, SGU_CHUNK, GW)
    vec_specs = [_const_spec((DEPTH, GW))] * 7

    T = ROW_TILE

    def row_tile(g):
        return (jnp.maximum(g - N_PREP, 0), 0)

    def weight_chunk(g):
        s = jnp.minimum(g, N_PREP - 1)
        return (s // PREP_CHUNKS, 0, s % PREP_CHUNKS)

    out = pl.pallas_call(
        _trunk_kernel,
        grid=(N_PREP + SEQ // T,),
        in_specs=[
            pl.BlockSpec((T, D_MODEL), row_tile),
            _const_spec((1, D_MODEL)),
            pl.BlockSpec((1, D_MODEL, PREP_IN_COLS), weight_chunk),
            _const_spec((DEPTH, 3 * D_MODEL)),
            _const_spec((DEPTH, D_MODEL)),
            _const_spec((1, D_MODEL)),
            pl.BlockSpec((1, D_MODEL, PREP_IN_COLS), weight_chunk),
            pl.BlockSpec((1, D_MODEL, PREP_OUT_COLS), weight_chunk),
            _const_spec((DEPTH, SHORT_CONV, GW)),
            _const_spec((DEPTH, CONF_WIDTH, GW)),
            *vec_specs,
            _const_spec((DEPTH, GW, GW)),
            _const_spec((DEPTH, GW, GW)),
            _const_spec((DEPTH, N_HEADS_D, SGU_CHUNK, SGU_CHUNK)),
            _const_spec((DEPTH, SGU_CHUNK, GW)),
        ],
        out_specs=pl.BlockSpec((T, D_MODEL), row_tile),
        out_shape=jax.ShapeDtypeStruct((SEQ, D_MODEL), f32),
        scratch_shapes=[
            pltpu.VMEM((DEPTH, N_CT, HEAD_A + T, LANES), f32),
            pltpu.VMEM((DEPTH, N_CT, HEAD_P + T, LANES), f32),
            pltpu.VMEM((DEPTH, N_CT, HEAD_P + T, LANES), f32),
            pltpu.VMEM((DEPTH, HEAD_P + T, LANES), f32),
            pltpu.VMEM((DEPTH, HEAD_P + T, LANES), f32),
            pltpu.VMEM((DEPTH, N_CT, HEAD_C + T, LANES), f32),
            pltpu.VMEM((DEPTH, 8, 3 * D_MODEL), f32),
            pltpu.VMEM((DEPTH, D_MODEL, D_IN), jnp.bfloat16),
            pltpu.VMEM((DEPTH, D_MODEL, D_MODEL), jnp.bfloat16),
        ],
        compiler_params=pltpu.CompilerParams(
            dimension_semantics=("arbitrary",),
            vmem_limit_bytes=VMEM_LIMIT_BYTES),
        name="hybrid_trunk",
    )(x.reshape(SEQ, D_MODEL), c, w_ada, b_ada, norm_g, final_g.reshape(1, D_MODEL),
      w_in, w_out, w_conv_a, w_dw_c,
      pool_scale, b_dw_c, ln_g_c, ln_b_c, b_pw2_c, ln_g_d, ln_b_d,
      w_pool_bd, w_pw2_c.astype(jnp.bfloat16), w_s_d, b_s_full)
    return out.reshape(1, SEQ, D_MODEL)
```

```python
import math

import jax
import jax.numpy as jnp
from jax import lax
from jax.experimental import pallas as pl
from jax.experimental.pallas import tpu as pltpu

D_MODEL = 1024
SEQ = 16384
DEPTH = 2
GW = 256
D_IN = 12 * GW
SHORT_CONV = 3
CONF_WIDTH = 31
POOL_WINDOWS = (2, 4, 8, 16)
POOL_GROUP_DIM = GW // len(POOL_WINDOWS)
SGU_CHUNK = 128
N_HEADS_D = 4
SGU_HEAD_DIM = GW // N_HEADS_D
EPS = 1e-6

LANES = 128
N_CT = GW // LANES
ROW_TILE = 512
HEAD_A = 8
HEAD_P = 8
HEAD_C = 32
VMEM_LIMIT_BYTES = 56 * 1024 * 1024
HALVED_GROUPS = (3, 5, 7, 8, 11)
PREP_CHUNKS = 4
N_PREP = DEPTH * PREP_CHUNKS
PREP_IN_COLS = D_IN // PREP_CHUNKS
PREP_OUT_COLS = D_MODEL // PREP_CHUNKS


def _silu(x):
    hx = 0.5 * x
    return hx + hx * jnp.tanh(hx)


def _silu_of_half(hx):
    return hx + hx * jnp.tanh(hx)


def _sigmoid_of_half(hx):
    return 0.5 + 0.5 * jnp.tanh(hx)


def _gelu_tanh(x):
    c = math.sqrt(2.0 / math.pi)
    hx = 0.5 * x
    return hx + hx * jnp.tanh(x * ((x * x) * (0.044715 * c) + c))


def _layer_norm(x, g, b):
    mu = jnp.mean(x, axis=-1, keepdims=True)
    xc = x - mu
    var = jnp.mean(xc * xc, axis=-1, keepdims=True)
    return xc * lax.rsqrt(var + EPS) * g + b


def _rms_norm(x, g):
    ms = jnp.mean(x * x, axis=-1, keepdims=True)
    return (x * lax.rsqrt(ms + EPS)) * g


def _trunk_kernel(x_ref, c_ref, w_ada_ref, b_ada_ref, norm_g_ref, final_g_ref, w_in_ref, w_out_ref,
                  w_conv_a_ref, w_dw_c_ref, pool_scale_ref, b_dw_c_ref, ln_g_c_ref, ln_b_c_ref, b_pw2_c_ref,
                  ln_g_d_ref, ln_b_d_ref, w_pool_ref, w_pw2_ref, w_s_ref, b_s_ref, o_ref,
                  cx_buf, p_buf, s2_buf, s4_buf, s8_buf, h_buf, mod_buf, w_in_buf, w_out_buf, w_s_buf):
    g = pl.program_id(0)
    T = ROW_TILE

    for s in range(N_PREP):
        l, j = divmod(s, PREP_CHUNKS)

        @pl.when(g == s)
        def _(l=l, j=j):
            in_cols = slice(j * PREP_IN_COLS, (j + 1) * PREP_IN_COLS)
            out_cols = slice(j * PREP_OUT_COLS, (j + 1) * PREP_OUT_COLS)
            first = j * PREP_IN_COLS // GW
            col_scale = jnp.concatenate(
                [jnp.full((1, GW), 0.5 if first + k in HALVED_GROUPS else 1.0, jnp.float32)
                 for k in range(PREP_IN_COLS // GW)], axis=1)
            w_in_buf[l, :, in_cols] = (w_in_ref[0] * col_scale).astype(jnp.bfloat16)
            w_out_buf[l, :, out_cols] = w_out_ref[0].astype(jnp.bfloat16)
            if j == 0:
                tri = (lax.broadcasted_iota(jnp.int32, (SGU_CHUNK, SGU_CHUNK), 0)
                       >= lax.broadcasted_iota(jnp.int32, (SGU_CHUNK, SGU_CHUNK), 1))
                w_s_buf[l] = jnp.concatenate([jnp.where(tri, w_s_ref[l, hd], 0.0).astype(jnp.bfloat16)
                                              for hd in range(N_HEADS_D)], axis=1)
            ca = _silu(jnp.broadcast_to(c_ref[...], (8, D_MODEL))).astype(jnp.bfloat16)
            mod_buf[l, :, in_cols] = (jnp.dot(ca, w_ada_ref[0].astype(jnp.bfloat16),
                                              preferred_element_type=jnp.float32)
                                      + b_ada_ref[l:l + 1, in_cols])

    @pl.when(g == 0)
    def _():
        cx_buf[:, :, 0:HEAD_A, :] = jnp.zeros((DEPTH, N_CT, HEAD_A, LANES), jnp.float32)
        for buf in (p_buf, s2_buf):
            buf[:, :, 0:HEAD_P, :] = jnp.zeros((DEPTH, N_CT, HEAD_P, LANES), jnp.float32)
        for buf in (s4_buf, s8_buf):
            buf[:, 0:HEAD_P, :] = jnp.zeros((DEPTH, HEAD_P, LANES), jnp.float32)
        h_buf[:, :, 0:HEAD_C, :] = jnp.zeros((DEPTH, N_CT, HEAD_C, LANES), jnp.float32)

    @pl.when(g >= N_PREP)
    def _():
        _trunk_step(g - N_PREP, x_ref, mod_buf, norm_g_ref, final_g_ref, w_in_buf, w_out_buf,
                    w_conv_a_ref, w_dw_c_ref, pool_scale_ref, b_dw_c_ref, ln_g_c_ref, ln_b_c_ref, b_pw2_c_ref,
                    ln_g_d_ref, ln_b_d_ref, w_pool_ref, w_pw2_ref, w_s_buf, b_s_ref, o_ref,
                    cx_buf, p_buf, s2_buf, s4_buf, s8_buf, h_buf)


def _trunk_step(i, x_ref, mod_ref, norm_g_ref, final_g_ref, w_in_ref, w_out_ref,
                w_conv_a_ref, w_dw_c_ref, pool_scale_ref, b_dw_c_ref, ln_g_c_ref, ln_b_c_ref, b_pw2_c_ref,
                ln_g_d_ref, ln_b_d_ref, w_pool_ref, w_pw2_ref, w_s_ref, b_s_ref, o_ref,
                cx_buf, p_buf, s2_buf, s4_buf, s8_buf, h_buf):
    T = ROW_TILE
    lane = lax.broadcasted_iota(jnp.int32, (T, LANES), 1)
    left = lane < POOL_GROUP_DIM
    top = max(POOL_WINDOWS)
    row1 = lax.broadcasted_iota(jnp.int32, (top, LANES), 0) + (i * T + 1)
    inv_count = []
    left_top = lax.broadcasted_iota(jnp.int32, (top, LANES), 1) < POOL_GROUP_DIM
    left_row = lax.broadcasted_iota(jnp.int32, (1, LANES), 1) < POOL_GROUP_DIM
    for ct in range(N_CT):
        w_left, w_right = POOL_WINDOWS[2 * ct], POOL_WINDOWS[2 * ct + 1]
        window = jnp.where(left_top, w_left, w_right)
        inv_count.append((1.0 / jnp.minimum(row1, window).astype(jnp.float32),
                          jnp.where(left_row, 1.0 / w_left, 1.0 / w_right)))
    head_of_lane = lax.broadcasted_iota(jnp.int32, (SGU_CHUNK, GW), 1) // SGU_HEAD_DIM

    def vec(l, ref):
        return ref[l:l + 1, :]

    def shifted(buf, idx, head, back):
        return buf[idx + (slice(head - back, head - back + T), slice(None))]

    def carry(buf, idx, head):
        buf[idx + (slice(0, head), slice(None))] = buf[idx + (slice(T, T + head), slice(None))]

    def layer(l, x_cur):
        shift = mod_ref[l, 0:1, 0:D_MODEL]
        scale = mod_ref[l, 0:1, D_MODEL:2 * D_MODEL]
        gate = mod_ref[l, 0:1, 2 * D_MODEL:3 * D_MODEL]
        h = (_rms_norm(x_cur, norm_g_ref[l:l + 1, :] * (1.0 + scale)) + shift).astype(jnp.bfloat16)

        def short_conv(a_b, a_c, a_x, a_gh):
            cx = a_c * a_x
            convs = []
            for ct in range(N_CT):
                lanes = slice(ct * LANES, (ct + 1) * LANES)
                cx_buf[l, ct, HEAD_A:HEAD_A + T, :] = cx[:, lanes]
                conv = None
                for j in range(SHORT_CONV):
                    term = (w_conv_a_ref[l, j:j + 1, lanes]
                            * shifted(cx_buf, (l, ct), HEAD_A, SHORT_CONV - 1 - j))
                    conv = term if conv is None else conv + term
                carry(cx_buf, (l, ct), HEAD_A)
                convs.append(conv)
            return (a_b * jnp.concatenate(convs, axis=-1)) * _silu_of_half(a_gh)

        def pooling(b_p, b_gh):
            pooled = []
            for ct in range(N_CT):
                p = b_p[:, ct * LANES:(ct + 1) * LANES]
                p_buf[l, ct, HEAD_P:HEAD_P + T, :] = p
                s2 = p + shifted(p_buf, (l, ct), HEAD_P, 1)
                s2_buf[l, ct, HEAD_P:HEAD_P + T, :] = s2
                s4 = s2 + shifted(s2_buf, (l, ct), HEAD_P, 2)
                carry(p_buf, (l, ct), HEAD_P)
                carry(s2_buf, (l, ct), HEAD_P)
                if ct == 0:
                    wsum = jnp.where(left, s2, s4)
                else:
                    s4_buf[l, HEAD_P:HEAD_P + T, :] = s4
                    s8 = s4 + shifted(s4_buf, (l,), HEAD_P, 4)
                    s8_buf[l, HEAD_P:HEAD_P + T, :] = s8
                    s16 = s8 + shifted(s8_buf, (l,), HEAD_P, 8)
                    carry(s4_buf, (l,), HEAD_P)
                    carry(s8_buf, (l,), HEAD_P)
                    wsum = jnp.where(left, s8, s16)
                inv_top, inv_rest = inv_count[ct]
                pooled.append(jnp.concatenate([wsum[0:top] * inv_top - p[0:top],
                                               wsum[top:] * inv_rest - p[top:]], axis=0))
            pooled = jnp.concatenate(pooled, axis=-1)
            y_b = jnp.dot(pooled.astype(jnp.bfloat16), w_pool_ref[l], preferred_element_type=jnp.float32)
            return (y_b * vec(l, pool_scale_ref)) * _silu_of_half(b_gh)

        def conformer(c_a, c_glh, c_gh):
            hgl = c_a * _sigmoid_of_half(c_glh)
            accs = []
            for ct in range(N_CT):
                lanes = slice(ct * LANES, (ct + 1) * LANES)
                h_buf[l, ct, HEAD_C:HEAD_C + T, :] = hgl[:, lanes]
                acc = None
                for j in range(CONF_WIDTH):
                    term = (w_dw_c_ref[l, j:j + 1, lanes]
                            * shifted(h_buf, (l, ct), HEAD_C, CONF_WIDTH - 1 - j))
                    acc = term if acc is None else acc + term
                carry(h_buf, (l, ct), HEAD_C)
                accs.append(acc)
            hc = jnp.concatenate(accs, axis=-1) + vec(l, b_dw_c_ref)
            hc = _silu(_layer_norm(hc, vec(l, ln_g_c_ref), vec(l, ln_b_c_ref)))
            y_c = jnp.dot(hc.astype(jnp.bfloat16), w_pw2_ref[l], preferred_element_type=jnp.float32)
            return (y_c + vec(l, b_pw2_c_ref)) * _silu_of_half(c_gh)

        def sgu(d_u, d_v, d_gh):
            u = _gelu_tanh(d_u)
            v = _layer_norm(_gelu_tanh(d_v), vec(l, ln_g_d_ref), vec(l, ln_b_d_ref))
            mixed_chunks = []
            for ck in range(T // SGU_CHUNK):
                v_ck = v[ck * SGU_CHUNK:(ck + 1) * SGU_CHUNK, :]
                stacked = jnp.concatenate([jnp.where(head_of_lane == hd, v_ck, 0.0).astype(jnp.bfloat16)
                                           for hd in range(N_HEADS_D)], axis=0)
                mixed_chunks.append(jnp.dot(w_s_ref[l], stacked, preferred_element_type=jnp.float32)
                                    + b_s_ref[l])
            mixed = jnp.concatenate(mixed_chunks, axis=0)
            return (u * mixed) * _silu_of_half(d_gh)

        z = jnp.dot(h, w_in_ref[l], preferred_element_type=jnp.float32)
        zs = [z[:, k * GW:(k + 1) * GW] for k in range(D_IN // GW)]
        y_cat = jnp.concatenate([short_conv(*zs[0:4]), pooling(*zs[4:6]), conformer(*zs[6:9]),
                                 sgu(*zs[9:12])], axis=-1).astype(jnp.bfloat16)
        y = jnp.dot(y_cat, w_out_ref[l], preferred_element_type=jnp.float32)
        return x_cur + gate * y

    x_cur = x_ref[...]
    for l in range(DEPTH):
        x_cur = layer(l, x_cur)
    o_ref[...] = _rms_norm(x_cur, final_g_ref[...])


def _const_spec(shape):
    zeros = (0,) * len(shape)
    return pl.BlockSpec(shape, lambda i: zeros, pipeline_mode=pl.Buffered(1))


def kernel(x, c, norm_g, w_ada, b_ada, w_in, w_conv_a, w_pool, pool_scale, w_dw_c, b_dw_c, ln_g_c, ln_b_c, w_pw2_c, b_pw2_c, ln_g_d, ln_b_d, w_s_d, b_s_d, w_out, final_g):
    f32 = jnp.float32
    assert x.shape == (1, SEQ, D_MODEL) and c.shape == (1, D_MODEL)
    assert POOL_WINDOWS == (2, 4, 8, 16) and N_CT == 2

    n_grp = len(POOL_WINDOWS)
    w_pool_bd = (w_pool[:, :, :, None, :] * jnp.eye(n_grp, dtype=f32)[None, :, None, :, None]
                 ).reshape(DEPTH, GW, GW).astype(jnp.bfloat16)
    b_s_full = jnp.broadcast_to(jnp.swapaxes(b_s_d, 1, 2)[:, :, :, None],
                                (DEPTH, SGU_CHUNK, N_HEADS_D, SGU_HEAD_DIM)).reshape(DEPTH, SGU_CHUNK, GW)
    vec_specs = [_const_spec((DEPTH, GW))] * 7

    T = ROW_TILE

    def row_tile(g):
        return (jnp.maximum(g - N_PREP, 0), 0)

    def weight_chunk(g):
        s = jnp.minimum(g, N_PREP - 1)
        return (s // PREP_CHUNKS, 0, s % PREP_CHUNKS)

    out = pl.pallas_call(
        _trunk_kernel,
        grid=(N_PREP + SEQ // T,),
        in_specs=[
            pl.BlockSpec((T, D_MODEL), row_tile),
            _const_spec((1, D_MODEL)),
            pl.BlockSpec((1, D_MODEL, PREP_IN_COLS), weight_chunk),
            _const_spec((DEPTH, 3 * D_MODEL)),
            _const_spec((DEPTH, D_MODEL)),
            _const_spec((1, D_MODEL)),
            pl.BlockSpec((1, D_MODEL, PREP_IN_COLS), weight_chunk),
            pl.BlockSpec((1, D_MODEL, PREP_OUT_COLS), weight_chunk),
            _const_spec((DEPTH, SHORT_CONV, GW)),
            _const_spec((DEPTH, CONF_WIDTH, GW)),
            *vec_specs,
            _const_spec((DEPTH, GW, GW)),
            _const_spec((DEPTH, GW, GW)),
            _const_spec((DEPTH, N_HEADS_D, SGU_CHUNK, SGU_CHUNK)),
            _const_spec((DEPTH, SGU_CHUNK, GW)),
        ],
        out_specs=pl.BlockSpec((T, D_MODEL), row_tile),
        out_shape=jax.ShapeDtypeStruct((SEQ, D_MODEL), f32),
        scratch_shapes=[
            pltpu.VMEM((DEPTH, N_CT, HEAD_A + T, LANES), f32),
            pltpu.VMEM((DEPTH, N_CT, HEAD_P + T, LANES), f32),
            pltpu.VMEM((DEPTH, N_CT, HEAD_P + T, LANES), f32),
            pltpu.VMEM((DEPTH, HEAD_P + T, LANES), f32),
            pltpu.VMEM((DEPTH, HEAD_P + T, LANES), f32),
            pltpu.VMEM((DEPTH, N_CT, HEAD_C + T, LANES), f32),
            pltpu.VMEM((DEPTH, 8, 3 * D_MODEL), f32),
            pltpu.VMEM((DEPTH, D_MODEL, D_IN), jnp.bfloat16),
            pltpu.VMEM((DEPTH, D_MODEL, D_MODEL), jnp.bfloat16),
            pltpu.VMEM((DEPTH, SGU_CHUNK, N_HEADS_D * SGU_CHUNK), jnp.bfloat16),
        ],
        compiler_params=pltpu.CompilerParams(
            dimension_semantics=("arbitrary",),
            vmem_limit_bytes=VMEM_LIMIT_BYTES),
        name="hybrid_trunk",
    )(x.reshape(SEQ, D_MODEL), c, w_ada, b_ada, norm_g, final_g.reshape(1, D_MODEL),
      w_in, w_out, w_conv_a, w_dw_c,
      pool_scale, b_dw_c, ln_g_c, ln_b_c, b_pw2_c, ln_g_d, ln_b_d,
      w_pool_bd, w_pw2_c.astype(jnp.bfloat16), w_s_d, b_s_full)
    return out.reshape(1, SEQ, D_MODEL)
```

```python
import math

import jax
import jax.numpy as jnp
from jax import lax
from jax.experimental import pallas as pl
from jax.experimental.pallas import tpu as pltpu

D_MODEL = 1024
SEQ = 16384
DEPTH = 2
GW = 256
D_IN = 12 * GW
SHORT_CONV = 3
CONF_WIDTH = 31
POOL_WINDOWS = (2, 4, 8, 16)
POOL_GROUP_DIM = GW // len(POOL_WINDOWS)
SGU_CHUNK = 128
N_HEADS_D = 4
SGU_HEAD_DIM = GW // N_HEADS_D
EPS = 1e-6

LANES = 128
N_CT = GW // LANES
ROW_TILE = 512
HEAD_A = 8
HEAD_P = 8
HEAD_C = 32
VMEM_LIMIT_BYTES = 56 * 1024 * 1024
HALVED_GROUPS = (3, 5, 7, 8, 11)
PREP_CHUNKS = 4
N_PREP = DEPTH * PREP_CHUNKS
PREP_IN_COLS = D_IN // PREP_CHUNKS
PREP_OUT_COLS = D_MODEL // PREP_CHUNKS


def _silu(x):
    hx = 0.5 * x
    return hx + hx * jnp.tanh(hx)


def _silu_of_half(hx):
    return hx + hx * jnp.tanh(hx)


def _sigmoid_of_half(hx):
    return 0.5 + 0.5 * jnp.tanh(hx)


def _gelu_tanh(x):
    c = math.sqrt(2.0 / math.pi)
    hx = 0.5 * x
    return hx + hx * jnp.tanh(x * ((x * x) * (0.044715 * c) + c))


def _layer_norm(x, g, b):
    mu = jnp.mean(x, axis=-1, keepdims=True)
    xc = x - mu
    var = jnp.mean(xc * xc, axis=-1, keepdims=True)
    return xc * lax.rsqrt(var + EPS) * g + b


def _rms_norm(x, g):
    ms = jnp.mean(x * x, axis=-1, keepdims=True)
    return (x * lax.rsqrt(ms + EPS)) * g


def _trunk_kernel(x_ref, c_ref, w_ada_ref, b_ada_ref, norm_g_ref, final_g_ref, w_in_ref, w_out_ref,
                  w_conv_a_ref, w_dw_c_ref, pool_scale_ref, b_dw_c_ref, ln_g_c_ref, ln_b_c_ref, b_pw2_c_ref,
                  ln_g_d_ref, ln_b_d_ref, w_pool_ref, w_pw2_ref, w_s_ref, b_s_ref, o_ref,
                  cx_buf, p_buf, s2_buf, s4_buf, s8_buf, h_buf, mod_buf, w_in_buf, w_out_buf, w_s_buf):
    g = pl.program_id(0)
    T = ROW_TILE

    for s in range(N_PREP):
        l, j = divmod(s, PREP_CHUNKS)

        @pl.when(g == s)
        def _(l=l, j=j):
            in_cols = slice(j * PREP_IN_COLS, (j + 1) * PREP_IN_COLS)
            out_cols = slice(j * PREP_OUT_COLS, (j + 1) * PREP_OUT_COLS)
            first = j * PREP_IN_COLS // GW
            col_scale = jnp.concatenate(
                [jnp.full((1, GW), 0.5 if first + k in HALVED_GROUPS else 1.0, jnp.float32)
                 for k in range(PREP_IN_COLS // GW)], axis=1)
            w_in_buf[l, :, in_cols] = (w_in_ref[0] * col_scale).astype(jnp.bfloat16)
            w_out_buf[l, :, out_cols] = w_out_ref[0].astype(jnp.bfloat16)
            if j == 0:
                tri = (lax.broadcasted_iota(jnp.int32, (SGU_CHUNK, SGU_CHUNK), 0)
                       >= lax.broadcasted_iota(jnp.int32, (SGU_CHUNK, SGU_CHUNK), 1))
                w_s_buf[l] = jnp.concatenate([jnp.where(tri, w_s_ref[l, hd], 0.0).astype(jnp.bfloat16)
                                              for hd in range(N_HEADS_D)], axis=1)
            ca = _silu(jnp.broadcast_to(c_ref[...], (8, D_MODEL))).astype(jnp.bfloat16)
            mod_buf[l, :, in_cols] = (jnp.dot(ca, w_ada_ref[0].astype(jnp.bfloat16),
                                              preferred_element_type=jnp.float32)
                                      + b_ada_ref[l:l + 1, in_cols])

    @pl.when(g == 0)
    def _():
        cx_buf[:, :, 0:HEAD_A, :] = jnp.zeros((DEPTH, N_CT, HEAD_A, LANES), jnp.float32)
        for buf in (p_buf, s2_buf):
            buf[:, :, 0:HEAD_P, :] = jnp.zeros((DEPTH, N_CT, HEAD_P, LANES), jnp.float32)
        for buf in (s4_buf, s8_buf):
            buf[:, 0:HEAD_P, :] = jnp.zeros((DEPTH, HEAD_P, LANES), jnp.float32)
        h_buf[:, :, 0:HEAD_C, :] = jnp.zeros((DEPTH, N_CT, HEAD_C, LANES), jnp.float32)

    @pl.when(g >= N_PREP)
    def _():
        _trunk_step(g - N_PREP, x_ref, mod_buf, norm_g_ref, final_g_ref, w_in_buf, w_out_buf,
                    w_conv_a_ref, w_dw_c_ref, pool_scale_ref, b_dw_c_ref, ln_g_c_ref, ln_b_c_ref, b_pw2_c_ref,
                    ln_g_d_ref, ln_b_d_ref, w_pool_ref, w_pw2_ref, w_s_buf, b_s_ref, o_ref,
                    cx_buf, p_buf, s2_buf, s4_buf, s8_buf, h_buf)


def _trunk_step(i, x_ref, mod_ref, norm_g_ref, final_g_ref, w_in_ref, w_out_ref,
                w_conv_a_ref, w_dw_c_ref, pool_scale_ref, b_dw_c_ref, ln_g_c_ref, ln_b_c_ref, b_pw2_c_ref,
                ln_g_d_ref, ln_b_d_ref, w_pool_ref, w_pw2_ref, w_s_ref, b_s_ref, o_ref,
                cx_buf, p_buf, s2_buf, s4_buf, s8_buf, h_buf):
    T = ROW_TILE
    lane = lax.broadcasted_iota(jnp.int32, (T, LANES), 1)
    left = lane < POOL_GROUP_DIM
    top = max(POOL_WINDOWS)
    row1 = lax.broadcasted_iota(jnp.int32, (top, LANES), 0) + (i * T + 1)
    inv_count = []
    left_top = lax.broadcasted_iota(jnp.int32, (top, LANES), 1) < POOL_GROUP_DIM
    left_row = lax.broadcasted_iota(jnp.int32, (1, LANES), 1) < POOL_GROUP_DIM
    for ct in range(N_CT):
        w_left, w_right = POOL_WINDOWS[2 * ct], POOL_WINDOWS[2 * ct + 1]
        window = jnp.where(left_top, w_left, w_right)
        inv_count.append((1.0 / jnp.minimum(row1, window).astype(jnp.float32),
                          jnp.where(left_row, 1.0 / w_left, 1.0 / w_right)))
    head_of_lane = lax.broadcasted_iota(jnp.int32, (SGU_CHUNK, GW), 1) // SGU_HEAD_DIM

    def vec(l, ref):
        return ref[pl.ds(l, 1), :]

    def shifted(buf, idx, head, back):
        return buf[idx + (slice(head - back, head - back + T), slice(None))]

    def carry(buf, idx, head):
        buf[idx + (slice(0, head), slice(None))] = buf[idx + (slice(T, T + head), slice(None))]

    def layer(l, x_cur):
        shift = mod_ref[l, 0:1, 0:D_MODEL]
        scale = mod_ref[l, 0:1, D_MODEL:2 * D_MODEL]
        gate = mod_ref[l, 0:1, 2 * D_MODEL:3 * D_MODEL]
        h = (_rms_norm(x_cur, vec(l, norm_g_ref) * (1.0 + scale)) + shift).astype(jnp.bfloat16)

        def short_conv(a_b, a_c, a_x, a_gh):
            cx = a_c * a_x
            convs = []
            for ct in range(N_CT):
                lanes = slice(ct * LANES, (ct + 1) * LANES)
                cx_buf[l, ct, HEAD_A:HEAD_A + T, :] = cx[:, lanes]
                conv = None
                for j in range(SHORT_CONV):
                    term = (w_conv_a_ref[l, j:j + 1, lanes]
                            * shifted(cx_buf, (l, ct), HEAD_A, SHORT_CONV - 1 - j))
                    conv = term if conv is None else conv + term
                carry(cx_buf, (l, ct), HEAD_A)
                convs.append(conv)
            return (a_b * jnp.concatenate(convs, axis=-1)) * _silu_of_half(a_gh)

        def pooling(b_p, b_gh):
            pooled = []
            for ct in range(N_CT):
                p = b_p[:, ct * LANES:(ct + 1) * LANES]
                p_buf[l, ct, HEAD_P:HEAD_P + T, :] = p
                s2 = p + shifted(p_buf, (l, ct), HEAD_P, 1)
                s2_buf[l, ct, HEAD_P:HEAD_P + T, :] = s2
                s4 = s2 + shifted(s2_buf, (l, ct), HEAD_P, 2)
                carry(p_buf, (l, ct), HEAD_P)
                carry(s2_buf, (l, ct), HEAD_P)
                if ct == 0:
                    wsum = jnp.where(left, s2, s4)
                else:
                    s4_buf[l, HEAD_P:HEAD_P + T, :] = s4
                    s8 = s4 + shifted(s4_buf, (l,), HEAD_P, 4)
                    s8_buf[l, HEAD_P:HEAD_P + T, :] = s8
                    s16 = s8 + shifted(s8_buf, (l,), HEAD_P, 8)
                    carry(s4_buf, (l,), HEAD_P)
                    carry(s8_buf, (l,), HEAD_P)
                    wsum = jnp.where(left, s8, s16)
                inv_top, inv_rest = inv_count[ct]
                pooled.append(jnp.concatenate([wsum[0:top] * inv_top - p[0:top],
                                               wsum[top:] * inv_rest - p[top:]], axis=0))
            pooled = jnp.concatenate(pooled, axis=-1)
            y_b = jnp.dot(pooled.astype(jnp.bfloat16), w_pool_ref[l], preferred_element_type=jnp.float32)
            return (y_b * vec(l, pool_scale_ref)) * _silu_of_half(b_gh)

        def conformer(c_a, c_glh, c_gh):
            hgl = c_a * _sigmoid_of_half(c_glh)
            accs = []
            for ct in range(N_CT):
                lanes = slice(ct * LANES, (ct + 1) * LANES)
                h_buf[l, ct, HEAD_C:HEAD_C + T, :] = hgl[:, lanes]
                acc = None
                for j in range(CONF_WIDTH):
                    term = (w_dw_c_ref[l, j:j + 1, lanes]
                            * shifted(h_buf, (l, ct), HEAD_C, CONF_WIDTH - 1 - j))
                    acc = term if acc is None else acc + term
                carry(h_buf, (l, ct), HEAD_C)
                accs.append(acc)
            hc = jnp.concatenate(accs, axis=-1) + vec(l, b_dw_c_ref)
            hc = _silu(_layer_norm(hc, vec(l, ln_g_c_ref), vec(l, ln_b_c_ref)))
            y_c = jnp.dot(hc.astype(jnp.bfloat16), w_pw2_ref[l], preferred_element_type=jnp.float32)
            return (y_c + vec(l, b_pw2_c_ref)) * _silu_of_half(c_gh)

        def sgu(d_u, d_v, d_gh):
            u = _gelu_tanh(d_u)
            v = _layer_norm(_gelu_tanh(d_v), vec(l, ln_g_d_ref), vec(l, ln_b_d_ref))
            mixed_chunks = []
            for ck in range(T // SGU_CHUNK):
                v_ck = v[ck * SGU_CHUNK:(ck + 1) * SGU_CHUNK, :]
                stacked = jnp.concatenate([jnp.where(head_of_lane == hd, v_ck, 0.0).astype(jnp.bfloat16)
                                           for hd in range(N_HEADS_D)], axis=0)
                mixed_chunks.append(jnp.dot(w_s_ref[l], stacked, preferred_element_type=jnp.float32)
                                    + b_s_ref[l])
            mixed = jnp.concatenate(mixed_chunks, axis=0)
            return (u * mixed) * _silu_of_half(d_gh)

        z = jnp.dot(h, w_in_ref[l], preferred_element_type=jnp.float32)
        zs = [z[:, k * GW:(k + 1) * GW] for k in range(D_IN // GW)]
        y_cat = jnp.concatenate([short_conv(*zs[0:4]), pooling(*zs[4:6]), conformer(*zs[6:9]),
                                 sgu(*zs[9:12])], axis=-1).astype(jnp.bfloat16)
        y = jnp.dot(y_cat, w_out_ref[l], preferred_element_type=jnp.float32)
        return x_cur + gate * y

    o_ref[...] = x_ref[...]

    def run_layer(l, carry):
        o_ref[...] = layer(l, o_ref[...])
        return carry

    lax.fori_loop(0, DEPTH, run_layer, 0)
    o_ref[...] = _rms_norm(o_ref[...], final_g_ref[...])


def _const_spec(shape):
    zeros = (0,) * len(shape)
    return pl.BlockSpec(shape, lambda i: zeros, pipeline_mode=pl.Buffered(1))


def kernel(x, c, norm_g, w_ada, b_ada, w_in, w_conv_a, w_pool, pool_scale, w_dw_c, b_dw_c, ln_g_c, ln_b_c, w_pw2_c, b_pw2_c, ln_g_d, ln_b_d, w_s_d, b_s_d, w_out, final_g):
    f32 = jnp.float32
    assert x.shape == (1, SEQ, D_MODEL) and c.shape == (1, D_MODEL)
    assert POOL_WINDOWS == (2, 4, 8, 16) and N_CT == 2

    n_grp = len(POOL_WINDOWS)
    w_pool_bd = (w_pool[:, :, :, None, :] * jnp.eye(n_grp, dtype=f32)[None, :, None, :, None]
                 ).reshape(DEPTH, GW, GW).astype(jnp.bfloat16)
    b_s_full = jnp.broadcast_to(jnp.swapaxes(b_s_d, 1, 2)[:, :, :, None],
                                (DEPTH, SGU_CHUNK, N_HEADS_D, SGU_HEAD_DIM)).reshape(DEPTH, SGU_CHUNK, GW)
    vec_specs = [_const_spec((DEPTH, GW))] * 7

    T = ROW_TILE

    def row_tile(g):
        return (jnp.maximum(g - N_PREP, 0), 0)

    def weight_chunk(g):
        s = jnp.minimum(g, N_PREP - 1)
        return (s // PREP_CHUNKS, 0, s % PREP_CHUNKS)

    out = pl.pallas_call(
        _trunk_kernel,
        grid=(N_PREP + SEQ // T,),
        in_specs=[
            pl.BlockSpec((T, D_MODEL), row_tile),
            _const_spec((1, D_MODEL)),
            pl.BlockSpec((1, D_MODEL, PREP_IN_COLS), weight_chunk),
            _const_spec((DEPTH, 3 * D_MODEL)),
            _const_spec((DEPTH, D_MODEL)),
            _const_spec((1, D_MODEL)),
            pl.BlockSpec((1, D_MODEL, PREP_IN_COLS), weight_chunk),
            pl.BlockSpec((1, D_MODEL, PREP_OUT_COLS), weight_chunk),
            _const_spec((DEPTH, SHORT_CONV, GW)),
            _const_spec((DEPTH, CONF_WIDTH, GW)),
            *vec_specs,
            _const_spec((DEPTH, GW, GW)),
            _const_spec((DEPTH, GW, GW)),
            _const_spec((DEPTH, N_HEADS_D, SGU_CHUNK, SGU_CHUNK)),
            _const_spec((DEPTH, SGU_CHUNK, GW)),
        ],
        out_specs=pl.BlockSpec((T, D_MODEL), row_tile),
        out_shape=jax.ShapeDtypeStruct((SEQ, D_MODEL), f32),
        scratch_shapes=[
            pltpu.VMEM((DEPTH, N_CT, HEAD_A + T, LANES), f32),
            pltpu.VMEM((DEPTH, N_CT, HEAD_P + T, LANES), f32),
            pltpu.VMEM((DEPTH, N_CT, HEAD_P + T, LANES), f32),
            pltpu.VMEM((DEPTH, HEAD_P + T, LANES), f32),
            pltpu.VMEM((DEPTH, HEAD_P + T, LANES), f32),
            pltpu.VMEM((DEPTH, N_CT, HEAD_C + T, LANES), f32),
            pltpu.VMEM((DEPTH, 8, 3 * D_MODEL), f32),
            pltpu.VMEM((DEPTH, D_MODEL, D_IN), jnp.bfloat16),
            pltpu.VMEM((DEPTH, D_MODEL, D_MODEL), jnp.bfloat16),
            pltpu.VMEM((DEPTH, SGU_CHUNK, N_HEADS_D * SGU_CHUNK), jnp.bfloat16),
        ],
        compiler_params=pltpu.CompilerParams(
            dimension_semantics=("arbitrary",),
            vmem_limit_bytes=VMEM_LIMIT_BYTES),
        name="hybrid_trunk",
    )(x.reshape(SEQ, D_MODEL), c, w_ada, b_ada, norm_g, final_g.reshape(1, D_MODEL),
      w_in, w_out, w_conv_a, w_dw_c,
      pool_scale, b_dw_c, ln_g_c, ln_b_c, b_pw2_c, ln_g_d, ln_b_d,
      w_pool_bd, w_pw2_c.astype(jnp.bfloat16), w_s_d, b_s_full)
    return out.reshape(1, SEQ, D_MODEL)
```

```python
import math

import jax
import jax.numpy as jnp
from jax import lax
from jax.experimental import pallas as pl
from jax.experimental.pallas import tpu as pltpu

D_MODEL = 1024
SEQ = 16384
DEPTH = 2
GW = 256
D_IN = 12 * GW
SHORT_CONV = 3
CONF_WIDTH = 31
POOL_WINDOWS = (2, 4, 8, 16)
POOL_GROUP_DIM = GW // len(POOL_WINDOWS)
SGU_CHUNK = 128
N_HEADS_D = 4
SGU_HEAD_DIM = GW // N_HEADS_D
EPS = 1e-6

LANES = 128
N_CT = GW // LANES
ROW_TILE = 512
HEAD_A = 8
HEAD_P = 8
HEAD_C = 32
VMEM_LIMIT_BYTES = 56 * 1024 * 1024
HALVED_GROUPS = (3, 5, 7, 8, 11)
PREP_CHUNKS = 4
N_PREP = DEPTH * PREP_CHUNKS
PREP_IN_COLS = D_IN // PREP_CHUNKS
Z_CHUNK_ORDER = (2, 3, 0, 1)
PREP_OUT_COLS = D_MODEL // PREP_CHUNKS


def _silu(x):
    hx = 0.5 * x
    return hx + hx * jnp.tanh(hx)


def _silu_of_half(hx):
    return hx + hx * jnp.tanh(hx)


def _sigmoid_of_half(hx):
    return 0.5 + 0.5 * jnp.tanh(hx)


def _gelu_tanh(x):
    c = math.sqrt(2.0 / math.pi)
    hx = 0.5 * x
    return hx + hx * jnp.tanh(x * ((x * x) * (0.044715 * c) + c))


def _layer_norm(x, g, b):
    mu = jnp.mean(x, axis=-1, keepdims=True)
    xc = x - mu
    var = jnp.mean(xc * xc, axis=-1, keepdims=True)
    return xc * lax.rsqrt(var + EPS) * g + b


def _rms_norm(x, g):
    ms = jnp.mean(x * x, axis=-1, keepdims=True)
    return (x * lax.rsqrt(ms + EPS)) * g


def _trunk_kernel(x_ref, c_ref, w_ada_ref, b_ada_ref, norm_g_ref, final_g_ref, w_in_ref, w_out_ref,
                  w_conv_a_ref, w_dw_c_ref, pool_scale_ref, b_dw_c_ref, ln_g_c_ref, ln_b_c_ref, b_pw2_c_ref,
                  ln_g_d_ref, ln_b_d_ref, w_pool_ref, w_pw2_ref, w_s_ref, b_s_ref, o_ref,
                  cx_buf, p_buf, s2_buf, s4_buf, s8_buf, h_buf, mod_buf, w_in_buf, w_out_buf, w_s_buf):
    g = pl.program_id(0)
    T = ROW_TILE

    for s in range(N_PREP):
        l, j = divmod(s, PREP_CHUNKS)

        @pl.when(g == s)
        def _(l=l, j=j):
            in_cols = slice(j * PREP_IN_COLS, (j + 1) * PREP_IN_COLS)
            out_cols = slice(j * PREP_OUT_COLS, (j + 1) * PREP_OUT_COLS)
            first = j * PREP_IN_COLS // GW
            col_scale = jnp.concatenate(
                [jnp.full((1, GW), 0.5 if first + k in HALVED_GROUPS else 1.0, jnp.float32)
                 for k in range(PREP_IN_COLS // GW)], axis=1)
            stored = Z_CHUNK_ORDER.index(j)
            w_in_buf[l, :, stored * PREP_IN_COLS:(stored + 1) * PREP_IN_COLS] = (
                w_in_ref[0] * col_scale).astype(jnp.bfloat16)
            w_out_buf[l, :, out_cols] = w_out_ref[0].astype(jnp.bfloat16)
            if j == 0:
                tri = (lax.broadcasted_iota(jnp.int32, (SGU_CHUNK, SGU_CHUNK), 0)
                       >= lax.broadcasted_iota(jnp.int32, (SGU_CHUNK, SGU_CHUNK), 1))
                w_s_buf[l] = jnp.concatenate([jnp.where(tri, w_s_ref[l, hd], 0.0).astype(jnp.bfloat16)
                                              for hd in range(N_HEADS_D)], axis=1)
            ca = _silu(jnp.broadcast_to(c_ref[...], (8, D_MODEL))).astype(jnp.bfloat16)
            mod_buf[l, :, in_cols] = (jnp.dot(ca, w_ada_ref[0].astype(jnp.bfloat16),
                                              preferred_element_type=jnp.float32)
                                      + b_ada_ref[l:l + 1, in_cols])

    @pl.when(g == 0)
    def _():
        cx_buf[:, :, 0:HEAD_A, :] = jnp.zeros((DEPTH, N_CT, HEAD_A, LANES), jnp.float32)
        for buf in (p_buf, s2_buf):
            buf[:, :, 0:HEAD_P, :] = jnp.zeros((DEPTH, N_CT, HEAD_P, LANES), jnp.float32)
        for buf in (s4_buf, s8_buf):
            buf[:, 0:HEAD_P, :] = jnp.zeros((DEPTH, HEAD_P, LANES), jnp.float32)
        h_buf[:, :, 0:HEAD_C, :] = jnp.zeros((DEPTH, N_CT, HEAD_C, LANES), jnp.float32)

    @pl.when(g >= N_PREP)
    def _():
        _trunk_step(g - N_PREP, x_ref, mod_buf, norm_g_ref, final_g_ref, w_in_buf, w_out_buf,
                    w_conv_a_ref, w_dw_c_ref, pool_scale_ref, b_dw_c_ref, ln_g_c_ref, ln_b_c_ref, b_pw2_c_ref,
                    ln_g_d_ref, ln_b_d_ref, w_pool_ref, w_pw2_ref, w_s_buf, b_s_ref, o_ref,
                    cx_buf, p_buf, s2_buf, s4_buf, s8_buf, h_buf)


def _trunk_step(i, x_ref, mod_ref, norm_g_ref, final_g_ref, w_in_ref, w_out_ref,
                w_conv_a_ref, w_dw_c_ref, pool_scale_ref, b_dw_c_ref, ln_g_c_ref, ln_b_c_ref, b_pw2_c_ref,
                ln_g_d_ref, ln_b_d_ref, w_pool_ref, w_pw2_ref, w_s_ref, b_s_ref, o_ref,
                cx_buf, p_buf, s2_buf, s4_buf, s8_buf, h_buf):
    T = ROW_TILE
    lane = lax.broadcasted_iota(jnp.int32, (T, LANES), 1)
    left = lane < POOL_GROUP_DIM
    top = max(POOL_WINDOWS)
    row1 = lax.broadcasted_iota(jnp.int32, (top, LANES), 0) + (i * T + 1)
    inv_count = []
    left_top = lax.broadcasted_iota(jnp.int32, (top, LANES), 1) < POOL_GROUP_DIM
    left_row = lax.broadcasted_iota(jnp.int32, (1, LANES), 1) < POOL_GROUP_DIM
    for ct in range(N_CT):
        w_left, w_right = POOL_WINDOWS[2 * ct], POOL_WINDOWS[2 * ct + 1]
        window = jnp.where(left_top, w_left, w_right)
        inv_count.append((1.0 / jnp.minimum(row1, window).astype(jnp.float32),
                          jnp.where(left_row, 1.0 / w_left, 1.0 / w_right)))
    head_of_lane = lax.broadcasted_iota(jnp.int32, (SGU_CHUNK, GW), 1) // SGU_HEAD_DIM

    def vec(l, ref):
        return ref[l:l + 1, :]

    def shifted(buf, idx, head, back):
        return buf[idx + (slice(head - back, head - back + T), slice(None))]

    def carry(buf, idx, head):
        buf[idx + (slice(0, head), slice(None))] = buf[idx + (slice(T, T + head), slice(None))]

    def layer(l, x_cur):
        shift = mod_ref[l, 0:1, 0:D_MODEL]
        scale = mod_ref[l, 0:1, D_MODEL:2 * D_MODEL]
        gate = mod_ref[l, 0:1, 2 * D_MODEL:3 * D_MODEL]
        h = (_rms_norm(x_cur, norm_g_ref[l:l + 1, :] * (1.0 + scale)) + shift).astype(jnp.bfloat16)

        def short_conv(a_b, a_c, a_x, a_gh):
            cx = a_c * a_x
            convs = []
            for ct in range(N_CT):
                lanes = slice(ct * LANES, (ct + 1) * LANES)
                cx_buf[l, ct, HEAD_A:HEAD_A + T, :] = cx[:, lanes]
                conv = None
                for j in range(SHORT_CONV):
                    term = (w_conv_a_ref[l, j:j + 1, lanes]
                            * shifted(cx_buf, (l, ct), HEAD_A, SHORT_CONV - 1 - j))
                    conv = term if conv is None else conv + term
                carry(cx_buf, (l, ct), HEAD_A)
                convs.append(conv)
            return (a_b * jnp.concatenate(convs, axis=-1)) * _silu_of_half(a_gh)

        def pooling(b_p, b_gh):
            pooled = []
            for ct in range(N_CT):
                p = b_p[:, ct * LANES:(ct + 1) * LANES]
                p_buf[l, ct, HEAD_P:HEAD_P + T, :] = p
                s2 = p + shifted(p_buf, (l, ct), HEAD_P, 1)
                s2_buf[l, ct, HEAD_P:HEAD_P + T, :] = s2
                s4 = s2 + shifted(s2_buf, (l, ct), HEAD_P, 2)
                carry(p_buf, (l, ct), HEAD_P)
                carry(s2_buf, (l, ct), HEAD_P)
                if ct == 0:
                    wsum = jnp.where(left, s2, s4)
                else:
                    s4_buf[l, HEAD_P:HEAD_P + T, :] = s4
                    s8 = s4 + shifted(s4_buf, (l,), HEAD_P, 4)
                    s8_buf[l, HEAD_P:HEAD_P + T, :] = s8
                    s16 = s8 + shifted(s8_buf, (l,), HEAD_P, 8)
                    carry(s4_buf, (l,), HEAD_P)
                    carry(s8_buf, (l,), HEAD_P)
                    wsum = jnp.where(left, s8, s16)
                inv_top, inv_rest = inv_count[ct]
                pooled.append(jnp.concatenate([wsum[0:top] * inv_top - p[0:top],
                                               wsum[top:] * inv_rest - p[top:]], axis=0))
            pooled = jnp.concatenate(pooled, axis=-1)
            y_b = jnp.dot(pooled.astype(jnp.bfloat16), w_pool_ref[l], preferred_element_type=jnp.float32)
            return (y_b * vec(l, pool_scale_ref)) * _silu_of_half(b_gh)

        def conformer(c_a, c_glh, c_gh):
            hgl = c_a * _sigmoid_of_half(c_glh)
            accs = []
            for ct in range(N_CT):
                lanes = slice(ct * LANES, (ct + 1) * LANES)
                h_buf[l, ct, HEAD_C:HEAD_C + T, :] = hgl[:, lanes]
                acc = None
                for j in range(CONF_WIDTH):
                    term = (w_dw_c_ref[l, j:j + 1, lanes]
                            * shifted(h_buf, (l, ct), HEAD_C, CONF_WIDTH - 1 - j))
                    acc = term if acc is None else acc + term
                carry(h_buf, (l, ct), HEAD_C)
                accs.append(acc)
            hc = jnp.concatenate(accs, axis=-1) + vec(l, b_dw_c_ref)
            hc = _silu(_layer_norm(hc, vec(l, ln_g_c_ref), vec(l, ln_b_c_ref)))
            y_c = jnp.dot(hc.astype(jnp.bfloat16), w_pw2_ref[l], preferred_element_type=jnp.float32)
            return (y_c + vec(l, b_pw2_c_ref)) * _silu_of_half(c_gh)

        def sgu(d_u, d_v, d_gh):
            u = _gelu_tanh(d_u)
            v = _layer_norm(_gelu_tanh(d_v), vec(l, ln_g_d_ref), vec(l, ln_b_d_ref))
            mixed_chunks = []
            for ck in range(T // SGU_CHUNK):
                v_ck = v[ck * SGU_CHUNK:(ck + 1) * SGU_CHUNK, :]
                stacked = jnp.concatenate([jnp.where(head_of_lane == hd, v_ck, 0.0).astype(jnp.bfloat16)
                                           for hd in range(N_HEADS_D)], axis=0)
                mixed_chunks.append(jnp.dot(w_s_ref[l], stacked, preferred_element_type=jnp.float32)
                                    + b_s_ref[l])
            mixed = jnp.concatenate(mixed_chunks, axis=0)
            return (u * mixed) * _silu_of_half(d_gh)

        z = jnp.dot(h, w_in_ref[l], preferred_element_type=jnp.float32)
        per_chunk = PREP_IN_COLS // GW
        stored_group = [Z_CHUNK_ORDER.index(k // per_chunk) * per_chunk + k % per_chunk
                        for k in range(D_IN // GW)]
        zs = [z[:, s * GW:(s + 1) * GW] for s in stored_group]
        y_c = conformer(*zs[6:9])
        y_d = sgu(*zs[9:12])
        y_a = short_conv(*zs[0:4])
        y_b = pooling(*zs[4:6])
        y_cat = jnp.concatenate([y_a, y_b, y_c, y_d], axis=-1).astype(jnp.bfloat16)
        y = jnp.dot(y_cat, w_out_ref[l], preferred_element_type=jnp.float32)
        return x_cur + gate * y

    x_cur = x_ref[...]
    for l in range(DEPTH):
        x_cur = layer(l, x_cur)
    o_ref[...] = _rms_norm(x_cur, final_g_ref[...])


def _const_spec(shape):
    zeros = (0,) * len(shape)
    return pl.BlockSpec(shape, lambda i: zeros, pipeline_mode=pl.Buffered(1))


def kernel(x, c, norm_g, w_ada, b_ada, w_in, w_conv_a, w_pool, pool_scale, w_dw_c, b_dw_c, ln_g_c, ln_b_c, w_pw2_c, b_pw2_c, ln_g_d, ln_b_d, w_s_d, b_s_d, w_out, final_g):
    f32 = jnp.float32
    assert x.shape == (1, SEQ, D_MODEL) and c.shape == (1, D_MODEL)
    assert POOL_WINDOWS == (2, 4, 8, 16) and N_CT == 2

    n_grp = len(POOL_WINDOWS)
    w_pool_bd = (w_pool[:, :, :, None, :] * jnp.eye(n_grp, dtype=f32)[None, :, None, :, None]
                 ).reshape(DEPTH, GW, GW).astype(jnp.bfloat16)
    b_s_full = jnp.broadcast_to(jnp.swapaxes(b_s_d, 1, 2)[:, :, :, None],
                                (DEPTH, SGU_CHUNK, N_HEADS_D, SGU_HEAD_DIM)).reshape(DEPTH, SGU_CHUNK, GW)
    vec_specs = [_const_spec((DEPTH, GW))] * 7

    T = ROW_TILE

    def row_tile(g):
        return (jnp.maximum(g - N_PREP, 0), 0)

    def weight_chunk(g):
        s = jnp.minimum(g, N_PREP - 1)
        return (s // PREP_CHUNKS, 0, s % PREP_CHUNKS)

    out = pl.pallas_call(
        _trunk_kernel,
        grid=(N_PREP + SEQ // T,),
        in_specs=[
            pl.BlockSpec((T, D_MODEL), row_tile),
            _const_spec((1, D_MODEL)),
            pl.BlockSpec((1, D_MODEL, PREP_IN_COLS), weight_chunk),
            _const_spec((DEPTH, 3 * D_MODEL)),
            _const_spec((DEPTH, D_MODEL)),
            _const_spec((1, D_MODEL)),
            pl.BlockSpec((1, D_MODEL, PREP_IN_COLS), weight_chunk),
            pl.BlockSpec((1, D_MODEL, PREP_OUT_COLS), weight_chunk),
            _const_spec((DEPTH, SHORT_CONV, GW)),
            _const_spec((DEPTH, CONF_WIDTH, GW)),
            *vec_specs,
            _const_spec((DEPTH, GW, GW)),
            _const_spec((DEPTH, GW, GW)),
            _const_spec((DEPTH, N_HEADS_D, SGU_CHUNK, SGU_CHUNK)),
            _const_spec((DEPTH, SGU_CHUNK, GW)),
        ],
        out_specs=pl.BlockSpec((T, D_MODEL), row_tile),
        out_shape=jax.ShapeDtypeStruct((SEQ, D_MODEL), f32),
        scratch_shapes=[
            pltpu.VMEM((DEPTH, N_CT, HEAD_A + T, LANES), f32),
            pltpu.VMEM((DEPTH, N_CT, HEAD_P + T, LANES), f32),
            pltpu.VMEM((DEPTH, N_CT, HEAD_P + T, LANES), f32),
            pltpu.VMEM((DEPTH, HEAD_P + T, LANES), f32),
            pltpu.VMEM((DEPTH, HEAD_P + T, LANES), f32),
            pltpu.VMEM((DEPTH, N_CT, HEAD_C + T, LANES), f32),
            pltpu.VMEM((DEPTH, 8, 3 * D_MODEL), f32),
            pltpu.VMEM((DEPTH, D_MODEL, D_IN), jnp.bfloat16),
            pltpu.VMEM((DEPTH, D_MODEL, D_MODEL), jnp.bfloat16),
            pltpu.VMEM((DEPTH, SGU_CHUNK, N_HEADS_D * SGU_CHUNK), jnp.bfloat16),
        ],
        compiler_params=pltpu.CompilerParams(
            dimension_semantics=("arbitrary",),
            vmem_limit_bytes=VMEM_LIMIT_BYTES),
        name="hybrid_trunk",
    )(x.reshape(SEQ, D_MODEL), c, w_ada, b_ada, norm_g, final_g.reshape(1, D_MODEL),
      w_in, w_out, w_conv_a, w_dw_c,
      pool_scale, b_dw_c, ln_g_c, ln_b_c, b_pw2_c, ln_g_d, ln_b_d,
      w_pool_bd, w_pw2_c.astype(jnp.bfloat16), w_s_d, b_s_full)
    return out.reshape(1, SEQ, D_MODEL)
```

```python
import math

import jax
import jax.numpy as jnp
from jax import lax
from jax.experimental import pallas as pl
from jax.experimental.pallas import tpu as pltpu

D_MODEL = 1024
SEQ = 16384
DEPTH = 2
GW = 256
D_IN = 12 * GW
SHORT_CONV = 3
CONF_WIDTH = 31
POOL_WINDOWS = (2, 4, 8, 16)
POOL_GROUP_DIM = GW // len(POOL_WINDOWS)
SGU_CHUNK = 128
N_HEADS_D = 4
SGU_HEAD_DIM = GW // N_HEADS_D
EPS = 1e-6

LANES = 128
N_CT = GW // LANES
ROW_TILE = 512
HEAD_A = 8
HEAD_P = 8
HEAD_C = 32
CONV_ROWS = 64
VMEM_LIMIT_BYTES = 56 * 1024 * 1024
HALVED_GROUPS = (3, 5, 7, 8, 11)
PREP_CHUNKS = 4
N_PREP = DEPTH * PREP_CHUNKS
PREP_IN_COLS = D_IN // PREP_CHUNKS
Z_GROUP_ORDER = (6, 7, 1, 2, 10, 9, 4, 0, 8, 11, 3, 5)
PREP_OUT_COLS = D_MODEL // PREP_CHUNKS


def _silu(x):
    hx = 0.5 * x
    return hx + hx * jnp.tanh(hx)


def _silu_of_half(hx):
    return hx + hx * jnp.tanh(hx)


def _sigmoid_of_half(hx):
    return 0.5 + 0.5 * jnp.tanh(hx)


def _gelu_tanh(x):
    c = math.sqrt(2.0 / math.pi)
    hx = 0.5 * x
    return hx + hx * jnp.tanh(x * ((x * x) * (0.044715 * c) + c))


def _layer_norm(x, g, b):
    mu = jnp.mean(x, axis=-1, keepdims=True)
    xc = x - mu
    var = jnp.mean(xc * xc, axis=-1, keepdims=True)
    return xc * lax.rsqrt(var + EPS) * g + b


def _rms_norm(x, g):
    ms = jnp.mean(x * x, axis=-1, keepdims=True)
    return (x * lax.rsqrt(ms + EPS)) * g


def _trunk_kernel(x_ref, c_ref, w_ada_ref, b_ada_ref, norm_g_ref, final_g_ref, w_in_ref, w_out_ref,
                  w_conv_a_ref, w_dw_c_ref, pool_scale_ref, b_dw_c_ref, ln_g_c_ref, ln_b_c_ref, b_pw2_c_ref,
                  ln_g_d_ref, ln_b_d_ref, w_pool_ref, w_pw2_ref, w_s_ref, b_s_ref, o_ref,
                  cx_buf, p_buf, s2_buf, s4_buf, s8_buf, h_buf, mod_buf, w_in_buf, w_out_buf, w_s_buf):
    g = pl.program_id(0)
    T = ROW_TILE

    for s in range(N_PREP):
        l, j = divmod(s, PREP_CHUNKS)

        @pl.when(g == s)
        def _(l=l, j=j):
            in_cols = slice(j * PREP_IN_COLS, (j + 1) * PREP_IN_COLS)
            out_cols = slice(j * PREP_OUT_COLS, (j + 1) * PREP_OUT_COLS)
            for k in range(PREP_IN_COLS // GW):
                grp = j * PREP_IN_COLS // GW + k
                stored = Z_GROUP_ORDER.index(grp)
                w_grp = w_in_ref[0, :, k * GW:(k + 1) * GW]
                if grp in HALVED_GROUPS:
                    w_grp = w_grp * 0.5
                w_in_buf[l, :, stored * GW:(stored + 1) * GW] = w_grp.astype(jnp.bfloat16)
            w_out_buf[l, :, out_cols] = w_out_ref[0].astype(jnp.bfloat16)
            if j == 0:
                tri = (lax.broadcasted_iota(jnp.int32, (SGU_CHUNK, SGU_CHUNK), 0)
                       >= lax.broadcasted_iota(jnp.int32, (SGU_CHUNK, SGU_CHUNK), 1))
                w_s_buf[l] = jnp.concatenate([jnp.where(tri, w_s_ref[l, hd], 0.0).astype(jnp.bfloat16)
                                              for hd in range(N_HEADS_D)], axis=1)
            ca = _silu(jnp.broadcast_to(c_ref[...], (8, D_MODEL))).astype(jnp.bfloat16)
            mod_buf[l, :, in_cols] = (jnp.dot(ca, w_ada_ref[0].astype(jnp.bfloat16),
                                              preferred_element_type=jnp.float32)
                                      + b_ada_ref[l:l + 1, in_cols])

    @pl.when(g == 0)
    def _():
        cx_buf[:, :, 0:HEAD_A, :] = jnp.zeros((DEPTH, N_CT, HEAD_A, LANES), jnp.float32)
        for buf in (p_buf, s2_buf):
            buf[:, :, 0:HEAD_P, :] = jnp.zeros((DEPTH, N_CT, HEAD_P, LANES), jnp.float32)
        for buf in (s4_buf, s8_buf):
            buf[:, 0:HEAD_P, :] = jnp.zeros((DEPTH, HEAD_P, LANES), jnp.float32)
        h_buf[:, :, 0:HEAD_C, :] = jnp.zeros((DEPTH, N_CT, HEAD_C, LANES), jnp.float32)

    @pl.when(g >= N_PREP)
    def _():
        _trunk_step(g - N_PREP, x_ref, mod_buf, norm_g_ref, final_g_ref, w_in_buf, w_out_buf,
                    w_conv_a_ref, w_dw_c_ref, pool_scale_ref, b_dw_c_ref, ln_g_c_ref, ln_b_c_ref, b_pw2_c_ref,
                    ln_g_d_ref, ln_b_d_ref, w_pool_ref, w_pw2_ref, w_s_buf, b_s_ref, o_ref,
                    cx_buf, p_buf, s2_buf, s4_buf, s8_buf, h_buf)


def _trunk_step(i, x_ref, mod_ref, norm_g_ref, final_g_ref, w_in_ref, w_out_ref,
                w_conv_a_ref, w_dw_c_ref, pool_scale_ref, b_dw_c_ref, ln_g_c_ref, ln_b_c_ref, b_pw2_c_ref,
                ln_g_d_ref, ln_b_d_ref, w_pool_ref, w_pw2_ref, w_s_ref, b_s_ref, o_ref,
                cx_buf, p_buf, s2_buf, s4_buf, s8_buf, h_buf):
    T = ROW_TILE
    lane = lax.broadcasted_iota(jnp.int32, (T, LANES), 1)
    left = lane < POOL_GROUP_DIM
    top = max(POOL_WINDOWS)
    row1 = lax.broadcasted_iota(jnp.int32, (top, LANES), 0) + (i * T + 1)
    inv_count = []
    left_top = lax.broadcasted_iota(jnp.int32, (top, LANES), 1) < POOL_GROUP_DIM
    left_row = lax.broadcasted_iota(jnp.int32, (1, LANES), 1) < POOL_GROUP_DIM
    for ct in range(N_CT):
        w_left, w_right = POOL_WINDOWS[2 * ct], POOL_WINDOWS[2 * ct + 1]
        window = jnp.where(left_top, w_left, w_right)
        inv_count.append((1.0 / jnp.minimum(row1, window).astype(jnp.float32),
                          jnp.where(left_row, 1.0 / w_left, 1.0 / w_right)))
    head_of_lane = lax.broadcasted_iota(jnp.int32, (SGU_CHUNK, GW), 1) // SGU_HEAD_DIM

    def vec(l, ref):
        return ref[l:l + 1, :]

    def shifted(buf, idx, head, back):
        return buf[idx + (slice(head - back, head - back + T), slice(None))]

    def carry(buf, idx, head):
        buf[idx + (slice(0, head), slice(None))] = buf[idx + (slice(T, T + head), slice(None))]

    def layer(l, x_cur):
        shift = mod_ref[l, 0:1, 0:D_MODEL]
        scale = mod_ref[l, 0:1, D_MODEL:2 * D_MODEL]
        gate = mod_ref[l, 0:1, 2 * D_MODEL:3 * D_MODEL]
        h = (_rms_norm(x_cur, norm_g_ref[l:l + 1, :] * (1.0 + scale)) + shift).astype(jnp.bfloat16)

        def short_conv(a_b, a_c, a_x, a_gh):
            cx = a_c * a_x
            convs = []
            for ct in range(N_CT):
                lanes = slice(ct * LANES, (ct + 1) * LANES)
                cx_buf[l, ct, HEAD_A:HEAD_A + T, :] = cx[:, lanes]
                conv = None
                for j in range(SHORT_CONV):
                    term = (w_conv_a_ref[l, j:j + 1, lanes]
                            * shifted(cx_buf, (l, ct), HEAD_A, SHORT_CONV - 1 - j))
                    conv = term if conv is None else conv + term
                carry(cx_buf, (l, ct), HEAD_A)
                convs.append(conv)
            return (a_b * jnp.concatenate(convs, axis=-1)) * _silu_of_half(a_gh)

        def pooling(b_p, b_gh):
            pooled = []
            for ct in range(N_CT):
                p = b_p[:, ct * LANES:(ct + 1) * LANES]
                p_buf[l, ct, HEAD_P:HEAD_P + T, :] = p
                s2 = p + shifted(p_buf, (l, ct), HEAD_P, 1)
                s2_buf[l, ct, HEAD_P:HEAD_P + T, :] = s2
                s4 = s2 + shifted(s2_buf, (l, ct), HEAD_P, 2)
                carry(p_buf, (l, ct), HEAD_P)
                carry(s2_buf, (l, ct), HEAD_P)
                if ct == 0:
                    wsum = jnp.where(left, s2, s4)
                else:
                    s4_buf[l, HEAD_P:HEAD_P + T, :] = s4
                    s8 = s4 + shifted(s4_buf, (l,), HEAD_P, 4)
                    s8_buf[l, HEAD_P:HEAD_P + T, :] = s8
                    s16 = s8 + shifted(s8_buf, (l,), HEAD_P, 8)
                    carry(s4_buf, (l,), HEAD_P)
                    carry(s8_buf, (l,), HEAD_P)
                    wsum = jnp.where(left, s8, s16)
                inv_top, inv_rest = inv_count[ct]
                pooled.append(jnp.concatenate([wsum[0:top] * inv_top - p[0:top],
                                               wsum[top:] * inv_rest - p[top:]], axis=0))
            pooled = jnp.concatenate(pooled, axis=-1)
            y_b = jnp.dot(pooled.astype(jnp.bfloat16), w_pool_ref[l], preferred_element_type=jnp.float32)
            return (y_b * vec(l, pool_scale_ref)) * _silu_of_half(b_gh)

        def conformer(c_a, c_glh, c_gh):
            hgl = c_a * _sigmoid_of_half(c_glh)
            accs = []
            for ct in range(N_CT):
                lanes = slice(ct * LANES, (ct + 1) * LANES)
                h_buf[l, ct, HEAD_C:HEAD_C + T, :] = hgl[:, lanes]
                blocks = []
                for r0 in range(0, T, CONV_ROWS):
                    acc = None
                    for j in range(CONF_WIDTH):
                        start = HEAD_C - (CONF_WIDTH - 1 - j) + r0
                        term = w_dw_c_ref[l, j:j + 1, lanes] * h_buf[l, ct, start:start + CONV_ROWS, :]
                        acc = term if acc is None else acc + term
                    blocks.append(acc)
                carry(h_buf, (l, ct), HEAD_C)
                accs.append(jnp.concatenate(blocks, axis=0))
            hc = jnp.concatenate(accs, axis=-1) + vec(l, b_dw_c_ref)
            hc = _silu(_layer_norm(hc, vec(l, ln_g_c_ref), vec(l, ln_b_c_ref)))
            y_c = jnp.dot(hc.astype(jnp.bfloat16), w_pw2_ref[l], preferred_element_type=jnp.float32)
            return (y_c + vec(l, b_pw2_c_ref)) * _silu_of_half(c_gh)

        def sgu(d_u, d_v, d_gh):
            u = _gelu_tanh(d_u)
            v = _layer_norm(_gelu_tanh(d_v), vec(l, ln_g_d_ref), vec(l, ln_b_d_ref))
            mixed_chunks = []
            for ck in range(T // SGU_CHUNK):
                v_ck = v[ck * SGU_CHUNK:(ck + 1) * SGU_CHUNK, :]
                stacked = jnp.concatenate([jnp.where(head_of_lane == hd, v_ck, 0.0).astype(jnp.bfloat16)
                                           for hd in range(N_HEADS_D)], axis=0)
                mixed_chunks.append(jnp.dot(w_s_ref[l], stacked, preferred_element_type=jnp.float32)
                                    + b_s_ref[l])
            mixed = jnp.concatenate(mixed_chunks, axis=0)
            return (u * mixed) * _silu_of_half(d_gh)

        z = jnp.dot(h, w_in_ref[l], preferred_element_type=jnp.float32)
        zs = [z[:, s * GW:(s + 1) * GW] for s in (Z_GROUP_ORDER.index(k) for k in range(D_IN // GW))]
        y_c = conformer(*zs[6:9])
        y_d = sgu(*zs[9:12])
        y_a = short_conv(*zs[0:4])
        y_b = pooling(*zs[4:6])
        y_cat = jnp.concatenate([y_a, y_b, y_c, y_d], axis=-1).astype(jnp.bfloat16)
        y = jnp.dot(y_cat, w_out_ref[l], preferred_element_type=jnp.float32)
        return x_cur + gate * y

    x_cur = x_ref[...]
    for l in range(DEPTH):
        x_cur = layer(l, x_cur)
    o_ref[...] = _rms_norm(x_cur, final_g_ref[...])


def _const_spec(shape):
    zeros = (0,) * len(shape)
    return pl.BlockSpec(shape, lambda i: zeros, pipeline_mode=pl.Buffered(1))


def kernel(x, c, norm_g, w_ada, b_ada, w_in, w_conv_a, w_pool, pool_scale, w_dw_c, b_dw_c, ln_g_c, ln_b_c, w_pw2_c, b_pw2_c, ln_g_d, ln_b_d, w_s_d, b_s_d, w_out, final_g):
    f32 = jnp.float32
    assert x.shape == (1, SEQ, D_MODEL) and c.shape == (1, D_MODEL)
    assert POOL_WINDOWS == (2, 4, 8, 16) and N_CT == 2

    n_grp = len(POOL_WINDOWS)
    w_pool_bd = (w_pool[:, :, :, None, :] * jnp.eye(n_grp, dtype=f32)[None, :, None, :, None]
                 ).reshape(DEPTH, GW, GW).astype(jnp.bfloat16)
    b_s_full = jnp.broadcast_to(jnp.swapaxes(b_s_d, 1, 2)[:, :, :, None],
                                (DEPTH, SGU_CHUNK, N_HEADS_D, SGU_HEAD_DIM)).reshape(DEPTH, SGU_CHUNK, GW)
    vec_specs = [_const_spec((DEPTH, GW))] * 7

    T = ROW_TILE

    def row_tile(g):
        return (jnp.maximum(g - N_PREP, 0), 0)

    def weight_chunk(g):
        s = jnp.minimum(g, N_PREP - 1)
        return (s // PREP_CHUNKS, 0, s % PREP_CHUNKS)

    out = pl.pallas_call(
        _trunk_kernel,
        grid=(N_PREP + SEQ // T,),
        in_specs=[
            pl.BlockSpec((T, D_MODEL), row_tile),
            _const_spec((1, D_MODEL)),
            pl.BlockSpec((1, D_MODEL, PREP_IN_COLS), weight_chunk),
            _const_spec((DEPTH, 3 * D_MODEL)),
            _const_spec((DEPTH, D_MODEL)),
            _const_spec((1, D_MODEL)),
            pl.BlockSpec((1, D_MODEL, PREP_IN_COLS), weight_chunk),
            pl.BlockSpec((1, D_MODEL, PREP_OUT_COLS), weight_chunk),
            _const_spec((DEPTH, SHORT_CONV, GW)),
            _const_spec((DEPTH, CONF_WIDTH, GW)),
            *vec_specs,
            _const_spec((DEPTH, GW, GW)),
            _const_spec((DEPTH, GW, GW)),
            _const_spec((DEPTH, N_HEADS_D, SGU_CHUNK, SGU_CHUNK)),
            _const_spec((DEPTH, SGU_CHUNK, GW)),
        ],
        out_specs=pl.BlockSpec((T, D_MODEL), row_tile),
        out_shape=jax.ShapeDtypeStruct((SEQ, D_MODEL), f32),
        scratch_shapes=[
            pltpu.VMEM((DEPTH, N_CT, HEAD_A + T, LANES), f32),
            pltpu.VMEM((DEPTH, N_CT, HEAD_P + T, LANES), f32),
            pltpu.VMEM((DEPTH, N_CT, HEAD_P + T, LANES), f32),
            pltpu.VMEM((DEPTH, HEAD_P + T, LANES), f32),
            pltpu.VMEM((DEPTH, HEAD_P + T, LANES), f32),
            pltpu.VMEM((DEPTH, N_CT, HEAD_C + T, LANES), f32),
            pltpu.VMEM((DEPTH, 8, 3 * D_MODEL), f32),
            pltpu.VMEM((DEPTH, D_MODEL, D_IN), jnp.bfloat16),
            pltpu.VMEM((DEPTH, D_MODEL, D_MODEL), jnp.bfloat16),
            pltpu.VMEM((DEPTH, SGU_CHUNK, N_HEADS_D * SGU_CHUNK), jnp.bfloat16),
        ],
        compiler_params=pltpu.CompilerParams(
            dimension_semantics=("arbitrary",),
            vmem_limit_bytes=VMEM_LIMIT_BYTES),
        name="hybrid_trunk",
    )(x.reshape(SEQ, D_MODEL), c, w_ada, b_ada, norm_g, final_g.reshape(1, D_MODEL),
      w_in, w_out, w_conv_a, w_dw_c,
      pool_scale, b_dw_c, ln_g_c, ln_b_c, b_pw2_c, ln_g_d, ln_b_d,
      w_pool_bd, w_pw2_c.astype(jnp.bfloat16), w_s_d, b_s_full)
    return out.reshape(1, SEQ, D_MODEL)
```

```python
import math

import jax
import jax.numpy as jnp
from jax import lax
from jax.experimental import pallas as pl
from jax.experimental.pallas import tpu as pltpu

D_MODEL = 1024
SEQ = 16384
DEPTH = 2
GW = 256
D_IN = 12 * GW
SHORT_CONV = 3
CONF_WIDTH = 31
POOL_WINDOWS = (2, 4, 8, 16)
POOL_GROUP_DIM = GW // len(POOL_WINDOWS)
SGU_CHUNK = 128
N_HEADS_D = 4
SGU_HEAD_DIM = GW // N_HEADS_D
EPS = 1e-6

SUBLANES = 8
LANES = 128
N_CT = GW // LANES
ROW_TILE = 512
HEAD_A = 8
HEAD_P = 8
HEAD_C = 32
CONV_ROWS = ROW_TILE
VMEM_LIMIT_BYTES = 56 * 1024 * 1024
HALVED_GROUPS = (3, 5, 7, 8, 11)
PREP_CHUNKS = 4
N_PREP = DEPTH * PREP_CHUNKS
PREP_IN_COLS = D_IN // PREP_CHUNKS
Z_GROUP_ORDER = (6, 7, 1, 2, 10, 9, 4, 0, 8, 11, 3, 5)
PREP_OUT_COLS = D_MODEL // PREP_CHUNKS


def _silu(x):
    hx = 0.5 * x
    return hx + hx * jnp.tanh(hx)


def _silu_of_half(hx):
    return hx + hx * jnp.tanh(hx)


def _sigmoid_of_half(hx):
    return 0.5 + 0.5 * jnp.tanh(hx)


def _gelu_tanh(x):
    c = math.sqrt(2.0 / math.pi)
    hx = 0.5 * x
    return hx + hx * jnp.tanh(x * ((x * x) * (0.044715 * c) + c))


def _layer_norm(x, g, b):
    mu = jnp.mean(x, axis=-1, keepdims=True)
    xc = x - mu
    var = jnp.mean(xc * xc, axis=-1, keepdims=True)
    return xc * lax.rsqrt(var + EPS) * g + b


def _rms_norm(x, g):
    ms = jnp.mean(x * x, axis=-1, keepdims=True)
    return (x * lax.rsqrt(ms + EPS)) * g


def _trunk_kernel(x_ref, c_ref, w_ada_ref, b_ada_ref, norm_g_ref, final_g_ref, w_in_ref, w_out_ref,
                  w_conv_a_ref, w_dw_c_ref, pool_scale_ref, b_dw_c_ref, ln_g_c_ref, ln_b_c_ref, b_pw2_c_ref,
                  ln_g_d_ref, ln_b_d_ref, w_pool_ref, w_pw2_ref, w_s_ref, b_s_ref, o_ref,
                  cx_buf, p_buf, s2_buf, s4_buf, s8_buf, h_buf, mod_buf, w_in_buf, w_out_buf, w_s_buf, w_pw2_buf):
    g = pl.program_id(0)
    T = ROW_TILE

    for s in range(N_PREP):
        l, j = divmod(s, PREP_CHUNKS)

        @pl.when(g == s)
        def _(l=l, j=j):
            in_cols = slice(j * PREP_IN_COLS, (j + 1) * PREP_IN_COLS)
            out_cols = slice(j * PREP_OUT_COLS, (j + 1) * PREP_OUT_COLS)
            for k in range(PREP_IN_COLS // GW):
                grp = j * PREP_IN_COLS // GW + k
                stored = Z_GROUP_ORDER.index(grp)
                w_grp = w_in_ref[0, :, k * GW:(k + 1) * GW]
                if grp in HALVED_GROUPS:
                    w_grp = w_grp * 0.5
                w_in_buf[l, :, stored * GW:(stored + 1) * GW] = w_grp.astype(jnp.bfloat16)
            w_out_buf[l, :, out_cols] = w_out_ref[0].astype(jnp.bfloat16)
            if j == 0:
                tri = (lax.broadcasted_iota(jnp.int32, (SGU_CHUNK, SGU_CHUNK), 0)
                       >= lax.broadcasted_iota(jnp.int32, (SGU_CHUNK, SGU_CHUNK), 1))
                w_pw2_buf[l] = w_pw2_ref[l].astype(jnp.bfloat16)
                w_s_buf[l] = jnp.concatenate([jnp.where(tri, w_s_ref[l, hd], 0.0).astype(jnp.bfloat16)
                                              for hd in range(N_HEADS_D)], axis=1)
            ca = _silu(jnp.broadcast_to(c_ref[...], (SUBLANES, D_MODEL))).astype(jnp.bfloat16)
            mod_buf[l, :, in_cols] = (jnp.dot(ca, w_ada_ref[0].astype(jnp.bfloat16),
                                              preferred_element_type=jnp.float32)
                                      + b_ada_ref[l:l + 1, in_cols])

    @pl.when(g == 0)
    def _():
        cx_buf[:, :, 0:HEAD_A, :] = jnp.zeros((DEPTH, N_CT, HEAD_A, LANES), jnp.float32)
        for buf in (p_buf, s2_buf):
            buf[:, :, 0:HEAD_P, :] = jnp.zeros((DEPTH, N_CT, HEAD_P, LANES), jnp.float32)
        for buf in (s4_buf, s8_buf):
            buf[:, 0:HEAD_P, :] = jnp.zeros((DEPTH, HEAD_P, LANES), jnp.float32)
        h_buf[:, :, 0:HEAD_C, :] = jnp.zeros((DEPTH, N_CT, HEAD_C, LANES), jnp.float32)

    @pl.when(g >= N_PREP)
    def _():
        _trunk_step(g - N_PREP, x_ref, mod_buf, norm_g_ref, final_g_ref, w_in_buf, w_out_buf,
                    w_conv_a_ref, w_dw_c_ref, pool_scale_ref, b_dw_c_ref, ln_g_c_ref, ln_b_c_ref, b_pw2_c_ref,
                    ln_g_d_ref, ln_b_d_ref, w_pool_ref, w_pw2_buf, w_s_buf, b_s_ref, o_ref,
                    cx_buf, p_buf, s2_buf, s4_buf, s8_buf, h_buf)


def _trunk_step(i, x_ref, mod_ref, norm_g_ref, final_g_ref, w_in_ref, w_out_ref,
                w_conv_a_ref, w_dw_c_ref, pool_scale_ref, b_dw_c_ref, ln_g_c_ref, ln_b_c_ref, b_pw2_c_ref,
                ln_g_d_ref, ln_b_d_ref, w_pool_ref, w_pw2_ref, w_s_ref, b_s_ref, o_ref,
                cx_buf, p_buf, s2_buf, s4_buf, s8_buf, h_buf):
    T = ROW_TILE
    lane = lax.broadcasted_iota(jnp.int32, (T, LANES), 1)
    left = lane < POOL_GROUP_DIM
    top = max(POOL_WINDOWS)
    row1 = lax.broadcasted_iota(jnp.int32, (top, LANES), 0) + (i * T + 1)
    inv_count = []
    left_top = lax.broadcasted_iota(jnp.int32, (top, LANES), 1) < POOL_GROUP_DIM
    left_row = lax.broadcasted_iota(jnp.int32, (1, LANES), 1) < POOL_GROUP_DIM
    for ct in range(N_CT):
        w_left, w_right = POOL_WINDOWS[2 * ct], POOL_WINDOWS[2 * ct + 1]
        window = jnp.where(left_top, w_left, w_right)
        inv_count.append((1.0 / jnp.minimum(row1, window).astype(jnp.float32),
                          jnp.where(left_row, 1.0 / w_left, 1.0 / w_right)))
    head_of_lane = lax.broadcasted_iota(jnp.int32, (SGU_CHUNK, GW), 1) // SGU_HEAD_DIM

    def vec(l, ref):
        return ref[l:l + 1, :]

    def shifted(buf, idx, head, back):
        return buf[idx + (slice(head - back, head - back + T), slice(None))]

    def carry(buf, idx, head):
        buf[idx + (slice(0, head), slice(None))] = buf[idx + (slice(T, T + head), slice(None))]

    def layer(l, x_cur):
        shift = mod_ref[l, 0:1, 0:D_MODEL]
        scale = mod_ref[l, 0:1, D_MODEL:2 * D_MODEL]
        gate = mod_ref[l, 0:1, 2 * D_MODEL:3 * D_MODEL]
        h = (_rms_norm(x_cur, norm_g_ref[l:l + 1, :] * (1.0 + scale)) + shift).astype(jnp.bfloat16)

        def short_conv(a_b, a_c, a_x, a_gh):
            cx = a_c * a_x
            convs = []
            for ct in range(N_CT):
                lanes = slice(ct * LANES, (ct + 1) * LANES)
                cx_buf[l, ct, HEAD_A:HEAD_A + T, :] = cx[:, lanes]
                conv = None
                for j in range(SHORT_CONV):
                    term = (w_conv_a_ref[l, j:j + 1, lanes]
                            * shifted(cx_buf, (l, ct), HEAD_A, SHORT_CONV - 1 - j))
                    conv = term if conv is None else conv + term
                carry(cx_buf, (l, ct), HEAD_A)
                convs.append(conv)
            return (a_b * jnp.concatenate(convs, axis=-1)) * _silu_of_half(a_gh)

        def pooling(b_p, b_gh):
            pooled = []
            for ct in range(N_CT):
                p = b_p[:, ct * LANES:(ct + 1) * LANES]
                p_buf[l, ct, HEAD_P:HEAD_P + T, :] = p
                s2 = p + shifted(p_buf, (l, ct), HEAD_P, 1)
                s2_buf[l, ct, HEAD_P:HEAD_P + T, :] = s2
                s4 = s2 + shifted(s2_buf, (l, ct), HEAD_P, 2)
                carry(p_buf, (l, ct), HEAD_P)
                carry(s2_buf, (l, ct), HEAD_P)
                if ct == 0:
                    wsum = jnp.where(left, s2, s4)
                else:
                    s4_buf[l, HEAD_P:HEAD_P + T, :] = s4
                    s8 = s4 + shifted(s4_buf, (l,), HEAD_P, 4)
                    s8_buf[l, HEAD_P:HEAD_P + T, :] = s8
                    s16 = s8 + shifted(s8_buf, (l,), HEAD_P, 8)
                    carry(s4_buf, (l,), HEAD_P)
                    carry(s8_buf, (l,), HEAD_P)
                    wsum = jnp.where(left, s8, s16)
                inv_top, inv_rest = inv_count[ct]
                pooled.append(jnp.concatenate([wsum[0:top] * inv_top - p[0:top],
                                               wsum[top:] * inv_rest - p[top:]], axis=0))
            pooled = jnp.concatenate(pooled, axis=-1)
            y_b = jnp.dot(pooled.astype(jnp.bfloat16), w_pool_ref[l], preferred_element_type=jnp.float32)
            return (y_b * vec(l, pool_scale_ref)) * _silu_of_half(b_gh)

        def conformer(c_a, c_glh, c_gh):
            hgl = c_a * _sigmoid_of_half(c_glh)
            accs = []
            for ct in range(N_CT):
                lanes = slice(ct * LANES, (ct + 1) * LANES)
                h_buf[l, ct, HEAD_C:HEAD_C + T, :] = hgl[:, lanes]
                blocks = []
                for r0 in range(0, T, CONV_ROWS):
                    acc = None
                    for j in range(CONF_WIDTH):
                        start = HEAD_C - (CONF_WIDTH - 1 - j) + r0
                        term = w_dw_c_ref[l, j:j + 1, lanes] * h_buf[l, ct, start:start + CONV_ROWS, :]
                        acc = term if acc is None else acc + term
                    blocks.append(acc)
                carry(h_buf, (l, ct), HEAD_C)
                accs.append(jnp.concatenate(blocks, axis=0))
            hc = jnp.concatenate(accs, axis=-1) + vec(l, b_dw_c_ref)
            hc = _silu(_layer_norm(hc, vec(l, ln_g_c_ref), vec(l, ln_b_c_ref)))
            y_c = jnp.dot(hc.astype(jnp.bfloat16), w_pw2_ref[l], preferred_element_type=jnp.float32)
            return (y_c + vec(l, b_pw2_c_ref)) * _silu_of_half(c_gh)

        def sgu(d_u, d_v, d_gh):
            u = _gelu_tanh(d_u)
            v = _layer_norm(_gelu_tanh(d_v), vec(l, ln_g_d_ref), vec(l, ln_b_d_ref))
            mixed_chunks = []
            for ck in range(T // SGU_CHUNK):
                v_ck = v[ck * SGU_CHUNK:(ck + 1) * SGU_CHUNK, :]
                stacked = jnp.concatenate([jnp.where(head_of_lane == hd, v_ck, 0.0).astype(jnp.bfloat16)
                                           for hd in range(N_HEADS_D)], axis=0)
                mixed_chunks.append(jnp.dot(w_s_ref[l], stacked, preferred_element_type=jnp.float32)
                                    + b_s_ref[l])
            mixed = jnp.concatenate(mixed_chunks, axis=0)
            return (u * mixed) * _silu_of_half(d_gh)

        z = jnp.dot(h, w_in_ref[l], preferred_element_type=jnp.float32)
        zs = [z[:, s * GW:(s + 1) * GW] for s in (Z_GROUP_ORDER.index(k) for k in range(D_IN // GW))]
        y_c = conformer(*zs[6:9])
        y_d = sgu(*zs[9:12])
        y_a = short_conv(*zs[0:4])
        y_b = pooling(*zs[4:6])
        y_cat = jnp.concatenate([y_a, y_b, y_c, y_d], axis=-1).astype(jnp.bfloat16)
        y = jnp.dot(y_cat, w_out_ref[l], preferred_element_type=jnp.float32)
        return x_cur + gate * y

    x_cur = x_ref[...]
    for l in range(DEPTH):
        x_cur = layer(l, x_cur)
    o_ref[...] = _rms_norm(x_cur, final_g_ref[...])


def _const_spec(shape):
    zeros = (0,) * len(shape)
    return pl.BlockSpec(shape, lambda i: zeros, pipeline_mode=pl.Buffered(1))


def kernel(x, c, norm_g, w_ada, b_ada, w_in, w_conv_a, w_pool, pool_scale, w_dw_c, b_dw_c, ln_g_c, ln_b_c, w_pw2_c, b_pw2_c, ln_g_d, ln_b_d, w_s_d, b_s_d, w_out, final_g):
    f32 = jnp.float32
    assert x.shape == (1, SEQ, D_MODEL) and c.shape == (1, D_MODEL)
    assert POOL_WINDOWS == (2, 4, 8, 16) and N_CT == 2

    n_grp = len(POOL_WINDOWS)
    w_pool_bd = (w_pool[:, :, :, None, :] * jnp.eye(n_grp, dtype=f32)[None, :, None, :, None]
                 ).reshape(DEPTH, GW, GW).astype(jnp.bfloat16)
    b_s_full = jnp.broadcast_to(jnp.swapaxes(b_s_d, 1, 2)[:, :, :, None],
                                (DEPTH, SGU_CHUNK, N_HEADS_D, SGU_HEAD_DIM)).reshape(DEPTH, SGU_CHUNK, GW)
    vec_specs = [_const_spec((DEPTH, GW))] * 7

    T = ROW_TILE

    def row_tile(g):
        return (jnp.maximum(g - N_PREP, 0), 0)

    def weight_chunk(g):
        s = jnp.minimum(g, N_PREP - 1)
        return (s // PREP_CHUNKS, 0, s % PREP_CHUNKS)

    out = pl.pallas_call(
        _trunk_kernel,
        grid=(N_PREP + SEQ // T,),
        in_specs=[
            pl.BlockSpec((T, D_MODEL), row_tile),
            _const_spec((1, D_MODEL)),
            pl.BlockSpec((1, D_MODEL, PREP_IN_COLS), weight_chunk),
            _const_spec((DEPTH, 3 * D_MODEL)),
            _const_spec((DEPTH, D_MODEL)),
            _const_spec((1, D_MODEL)),
            pl.BlockSpec((1, D_MODEL, PREP_IN_COLS), weight_chunk),
            pl.BlockSpec((1, D_MODEL, PREP_OUT_COLS), weight_chunk),
            _const_spec((DEPTH, SHORT_CONV, GW)),
            _const_spec((DEPTH, CONF_WIDTH, GW)),
            *vec_specs,
            _const_spec((DEPTH, GW, GW)),
            _const_spec((DEPTH, GW, GW)),
            _const_spec((DEPTH, N_HEADS_D, SGU_CHUNK, SGU_CHUNK)),
            _const_spec((DEPTH, SGU_CHUNK, GW)),
        ],
        out_specs=pl.BlockSpec((T, D_MODEL), row_tile),
        out_shape=jax.ShapeDtypeStruct((SEQ, D_MODEL), f32),
        scratch_shapes=[
            pltpu.VMEM((DEPTH, N_CT, HEAD_A + T, LANES), f32),
            pltpu.VMEM((DEPTH, N_CT, HEAD_P + T, LANES), f32),
            pltpu.VMEM((DEPTH, N_CT, HEAD_P + T, LANES), f32),
            pltpu.VMEM((DEPTH, HEAD_P + T, LANES), f32),
            pltpu.VMEM((DEPTH, HEAD_P + T, LANES), f32),
            pltpu.VMEM((DEPTH, N_CT, HEAD_C + T, LANES), f32),
            pltpu.VMEM((DEPTH, SUBLANES, 3 * D_MODEL), f32),
            pltpu.VMEM((DEPTH, D_MODEL, D_IN), jnp.bfloat16),
            pltpu.VMEM((DEPTH, D_MODEL, D_MODEL), jnp.bfloat16),
            pltpu.VMEM((DEPTH, SGU_CHUNK, N_HEADS_D * SGU_CHUNK), jnp.bfloat16),
            pltpu.VMEM((DEPTH, GW, GW), jnp.bfloat16),
        ],
        compiler_params=pltpu.CompilerParams(
            dimension_semantics=("arbitrary",),
            vmem_limit_bytes=VMEM_LIMIT_BYTES),
        name="hybrid_trunk",
    )(x.reshape(SEQ, D_MODEL), c, w_ada, b_ada, norm_g, final_g.reshape(1, D_MODEL),
      w_in, w_out, w_conv_a, w_dw_c,
      pool_scale, b_dw_c, ln_g_c, ln_b_c, b_pw2_c, ln_g_d, ln_b_d,
      w_pool_bd, w_pw2_c, w_s_d, b_s_full)
    return out.reshape(1, SEQ, D_MODEL)
```

```python
import math

import jax
import jax.numpy as jnp
from jax import lax
from jax.experimental import pallas as pl
from jax.experimental.pallas import tpu as pltpu

D_MODEL = 1024
SEQ = 16384
DEPTH = 2
GW = 256
D_IN = 12 * GW
SHORT_CONV = 3
CONF_WIDTH = 31
POOL_WINDOWS = (2, 4, 8, 16)
POOL_GROUP_DIM = GW // len(POOL_WINDOWS)
SGU_CHUNK = 128
N_HEADS_D = 4
SGU_HEAD_DIM = GW // N_HEADS_D
EPS = 1e-6

SUBLANES = 8
LANES = 128
N_CT = GW // LANES
ROW_TILE = 512
HEAD_A = 8
HEAD_P = 8
HEAD_C = 32
VMEM_LIMIT_BYTES = 56 * 1024 * 1024
HALVED_GROUPS = (3, 5, 7, 8, 11)
PREP_CHUNKS = 4
N_PREP = DEPTH * PREP_CHUNKS
PREP_IN_COLS = D_IN // PREP_CHUNKS
PREP_OUT_COLS = D_MODEL // PREP_CHUNKS
Z_GROUP_ORDER = (6, 7, 1, 2, 10, 9, 4, 0, 8, 11, 3, 5)


def _silu(x):
    hx = 0.5 * x
    return hx + hx * jnp.tanh(hx)


def _silu_of_half(hx):
    return hx + hx * jnp.tanh(hx)


def _sigmoid_of_half(hx):
    return 0.5 + 0.5 * jnp.tanh(hx)


def _gelu_tanh(x):
    c = math.sqrt(2.0 / math.pi)
    hx = 0.5 * x
    return hx + hx * jnp.tanh(x * ((x * x) * (0.044715 * c) + c))


def _layer_norm(x, g, b):
    mu = jnp.mean(x, axis=-1, keepdims=True)
    xc = x - mu
    var = jnp.mean(xc * xc, axis=-1, keepdims=True)
    return xc * lax.rsqrt(var + EPS) * g + b


def _rms_norm(x, g):
    ms = jnp.mean(x * x, axis=-1, keepdims=True)
    return (x * lax.rsqrt(ms + EPS)) * g


def _trunk_kernel(x_ref, c_ref, w_ada_ref, b_ada_ref, norm_g_ref, final_g_ref, w_in_ref, w_out_ref,
                  w_conv_a_ref, w_dw_c_ref, pool_scale_ref, b_dw_c_ref, ln_g_c_ref, ln_b_c_ref, b_pw2_c_ref,
                  ln_g_d_ref, ln_b_d_ref, w_pool_ref, w_pw2_ref, w_s_ref, b_s_ref, o_ref,
                  cx_buf, p_buf, s2_buf, s4_buf, s8_buf, h_buf, mod_buf, w_in_buf, w_out_buf, w_s_buf, w_pw2_buf):
    g = pl.program_id(0)
    T = ROW_TILE

    for s in range(N_PREP):
        l, j = divmod(s, PREP_CHUNKS)

        @pl.when(g == s)
        def _(l=l, j=j):
            in_cols = slice(j * PREP_IN_COLS, (j + 1) * PREP_IN_COLS)
            out_cols = slice(j * PREP_OUT_COLS, (j + 1) * PREP_OUT_COLS)
            for k in range(PREP_IN_COLS // GW):
                grp = j * PREP_IN_COLS // GW + k
                stored = Z_GROUP_ORDER.index(grp)
                w_grp = w_in_ref[0, :, k * GW:(k + 1) * GW]
                if grp in HALVED_GROUPS:
                    w_grp = w_grp * 0.5
                w_in_buf[l, :, stored * GW:(stored + 1) * GW] = w_grp.astype(jnp.bfloat16)
            w_out_buf[l, :, out_cols] = w_out_ref[0].astype(jnp.bfloat16)
            if j == 0:
                tri = (lax.broadcasted_iota(jnp.int32, (SGU_CHUNK, SGU_CHUNK), 0)
                       >= lax.broadcasted_iota(jnp.int32, (SGU_CHUNK, SGU_CHUNK), 1))
                w_pw2_buf[l] = w_pw2_ref[l].astype(jnp.bfloat16)
                w_s_buf[l] = jnp.concatenate([jnp.where(tri, w_s_ref[l, hd], 0.0).astype(jnp.bfloat16)
                                              for hd in range(N_HEADS_D)], axis=1)
            ca = _silu(jnp.broadcast_to(c_ref[...], (SUBLANES, D_MODEL))).astype(jnp.bfloat16)
            mod_buf[l, :, in_cols] = (jnp.dot(ca, w_ada_ref[0].astype(jnp.bfloat16),
                                              preferred_element_type=jnp.float32)
                                      + b_ada_ref[l:l + 1, in_cols])

    @pl.when(g == 0)
    def _():
        cx_buf[:, :, 0:HEAD_A, :] = jnp.zeros((DEPTH, N_CT, HEAD_A, LANES), jnp.float32)
        for buf in (p_buf, s2_buf):
            buf[:, :, 0:HEAD_P, :] = jnp.zeros((DEPTH, N_CT, HEAD_P, LANES), jnp.float32)
        for buf in (s4_buf, s8_buf):
            buf[:, 0:HEAD_P, :] = jnp.zeros((DEPTH, HEAD_P, LANES), jnp.float32)
        h_buf[:, :, 0:HEAD_C, :] = jnp.zeros((DEPTH, N_CT, HEAD_C, LANES), jnp.float32)

    @pl.when(g >= N_PREP)
    def _():
        _trunk_step(g - N_PREP, x_ref, mod_buf, norm_g_ref, final_g_ref, w_in_buf, w_out_buf,
                    w_conv_a_ref, w_dw_c_ref, pool_scale_ref, b_dw_c_ref, ln_g_c_ref, ln_b_c_ref, b_pw2_c_ref,
                    ln_g_d_ref, ln_b_d_ref, w_pool_ref, w_pw2_buf, w_s_buf, b_s_ref, o_ref,
                    cx_buf, p_buf, s2_buf, s4_buf, s8_buf, h_buf)


def _trunk_step(i, x_ref, mod_ref, norm_g_ref, final_g_ref, w_in_ref, w_out_ref,
                w_conv_a_ref, w_dw_c_ref, pool_scale_ref, b_dw_c_ref, ln_g_c_ref, ln_b_c_ref, b_pw2_c_ref,
                ln_g_d_ref, ln_b_d_ref, w_pool_ref, w_pw2_ref, w_s_ref, b_s_ref, o_ref,
                cx_buf, p_buf, s2_buf, s4_buf, s8_buf, h_buf):
    T = ROW_TILE
    lane = lax.broadcasted_iota(jnp.int32, (T, LANES), 1)
    left = lane < POOL_GROUP_DIM
    top = max(POOL_WINDOWS)
    row1 = lax.broadcasted_iota(jnp.int32, (top, LANES), 0) + (i * T + 1)
    inv_count = []
    left_top = lax.broadcasted_iota(jnp.int32, (top, LANES), 1) < POOL_GROUP_DIM
    left_row = lax.broadcasted_iota(jnp.int32, (1, LANES), 1) < POOL_GROUP_DIM
    for ct in range(N_CT):
        w_left, w_right = POOL_WINDOWS[2 * ct], POOL_WINDOWS[2 * ct + 1]
        window = jnp.where(left_top, w_left, w_right)
        inv_count.append((1.0 / jnp.minimum(row1, window).astype(jnp.float32),
                          jnp.where(left_row, 1.0 / w_left, 1.0 / w_right)))
    head_of_lane = lax.broadcasted_iota(jnp.int32, (SGU_CHUNK, GW), 1) // SGU_HEAD_DIM

    def vec(l, ref):
        return ref[l:l + 1, :]

    def shifted(buf, idx, head, back):
        return buf[idx + (slice(head - back, head - back + T), slice(None))]

    def carry(buf, idx, head):
        buf[idx + (slice(0, head), slice(None))] = buf[idx + (slice(T, T + head), slice(None))]

    def layer(l, x_cur):
        shift = mod_ref[l, 0:1, 0:D_MODEL]
        scale = mod_ref[l, 0:1, D_MODEL:2 * D_MODEL]
        gate = mod_ref[l, 0:1, 2 * D_MODEL:3 * D_MODEL]
        h = (_rms_norm(x_cur, norm_g_ref[l:l + 1, :] * (1.0 + scale)) + shift).astype(jnp.bfloat16)

        def short_conv(a_b, a_c, a_x, a_gh):
            cx = a_c * a_x
            convs = []
            for ct in range(N_CT):
                lanes = slice(ct * LANES, (ct + 1) * LANES)
                cx_buf[l, ct, HEAD_A:HEAD_A + T, :] = cx[:, lanes]
                conv = None
                for j in range(SHORT_CONV):
                    term = (w_conv_a_ref[l, j:j + 1, lanes]
                            * shifted(cx_buf, (l, ct), HEAD_A, SHORT_CONV - 1 - j))
                    conv = term if conv is None else conv + term
                carry(cx_buf, (l, ct), HEAD_A)
                convs.append(conv)
            return (a_b * jnp.concatenate(convs, axis=-1)) * _silu_of_half(a_gh)

        def pooling(b_p, b_gh):
            pooled = []
            for ct in range(N_CT):
                p = b_p[:, ct * LANES:(ct + 1) * LANES]
                p_buf[l, ct, HEAD_P:HEAD_P + T, :] = p
                s2 = p + shifted(p_buf, (l, ct), HEAD_P, 1)
                s2_buf[l, ct, HEAD_P:HEAD_P + T, :] = s2
                s4 = s2 + shifted(s2_buf, (l, ct), HEAD_P, 2)
                carry(p_buf, (l, ct), HEAD_P)
                carry(s2_buf, (l, ct), HEAD_P)
                if ct == 0:
                    wsum = jnp.where(left, s2, s4)
                else:
                    s4_buf[l, HEAD_P:HEAD_P + T, :] = s4
                    s8 = s4 + shifted(s4_buf, (l,), HEAD_P, 4)
                    s8_buf[l, HEAD_P:HEAD_P + T, :] = s8
                    s16 = s8 + shifted(s8_buf, (l,), HEAD_P, 8)
                    carry(s4_buf, (l,), HEAD_P)
                    carry(s8_buf, (l,), HEAD_P)
                    wsum = jnp.where(left, s8, s16)
                inv_top, inv_rest = inv_count[ct]
                pooled.append(jnp.concatenate([wsum[0:top] * inv_top - p[0:top],
                                               wsum[top:] * inv_rest - p[top:]], axis=0))
            pooled = jnp.concatenate(pooled, axis=-1)
            y_b = jnp.dot(pooled.astype(jnp.bfloat16), w_pool_ref[l], preferred_element_type=jnp.float32)
            return (y_b * vec(l, pool_scale_ref)) * _silu_of_half(b_gh)

        def conformer(c_a, c_glh, c_gh):
            hgl = c_a * _sigmoid_of_half(c_glh)
            accs = []
            for ct in range(N_CT):
                lanes = slice(ct * LANES, (ct + 1) * LANES)
                h_buf[l, ct, HEAD_C:HEAD_C + T, :] = hgl[:, lanes]
                acc = None
                for j in range(CONF_WIDTH):
                    term = (w_dw_c_ref[l, j:j + 1, lanes]
                            * shifted(h_buf, (l, ct), HEAD_C, CONF_WIDTH - 1 - j))
                    acc = term if acc is None else acc + term
                carry(h_buf, (l, ct), HEAD_C)
                accs.append(acc)
            hc = jnp.concatenate(accs, axis=-1) + vec(l, b_dw_c_ref)
            hc = _silu(_layer_norm(hc, vec(l, ln_g_c_ref), vec(l, ln_b_c_ref)))
            y_c = jnp.dot(hc.astype(jnp.bfloat16), w_pw2_ref[l], preferred_element_type=jnp.float32)
            return (y_c + vec(l, b_pw2_c_ref)) * _silu_of_half(c_gh)

        def sgu(d_u, d_v, d_gh):
            u = _gelu_tanh(d_u)
            v = _layer_norm(_gelu_tanh(d_v), vec(l, ln_g_d_ref), vec(l, ln_b_d_ref))
            mixed_chunks = []
            for ck in range(T // SGU_CHUNK):
                v_ck = v[ck * SGU_CHUNK:(ck + 1) * SGU_CHUNK, :]
                stacked = jnp.concatenate([jnp.where(head_of_lane == hd, v_ck, 0.0).astype(jnp.bfloat16)
                                           for hd in range(N_HEADS_D)], axis=0)
                mixed_chunks.append(jnp.dot(w_s_ref[l], stacked, preferred_element_type=jnp.float32)
                                    + b_s_ref[l])
            mixed = jnp.concatenate(mixed_chunks, axis=0)
            return (u * mixed) * _silu_of_half(d_gh)

        z = jnp.dot(h, w_in_ref[l], preferred_element_type=jnp.float32)
        zs = [z[:, s * GW:(s + 1) * GW] for s in (Z_GROUP_ORDER.index(k) for k in range(D_IN // GW))]
        y_c = conformer(*zs[6:9])
        y_d = sgu(*zs[9:12])
        y_a = short_conv(*zs[0:4])
        y_b = pooling(*zs[4:6])
        y_cat = jnp.concatenate([y_a, y_b, y_c, y_d], axis=-1).astype(jnp.bfloat16)
        y = jnp.dot(y_cat, w_out_ref[l], preferred_element_type=jnp.float32)
        return x_cur + gate * y

    x_cur = x_ref[...]
    for l in range(DEPTH):
        x_cur = layer(l, x_cur)
    o_ref[...] = _rms_norm(x_cur, final_g_ref[...])


def _const_spec(shape):
    zeros = (0,) * len(shape)
    return pl.BlockSpec(shape, lambda i: zeros, pipeline_mode=pl.Buffered(1))


def kernel(x, c, norm_g, w_ada, b_ada, w_in, w_conv_a, w_pool, pool_scale, w_dw_c, b_dw_c, ln_g_c, ln_b_c, w_pw2_c, b_pw2_c, ln_g_d, ln_b_d, w_s_d, b_s_d, w_out, final_g):
    f32 = jnp.float32
    assert x.shape == (1, SEQ, D_MODEL) and c.shape == (1, D_MODEL)
    assert POOL_WINDOWS == (2, 4, 8, 16) and N_CT == 2

    n_grp = len(POOL_WINDOWS)
    w_pool_bd = (w_pool[:, :, :, None, :] * jnp.eye(n_grp, dtype=f32)[None, :, None, :, None]
                 ).reshape(DEPTH, GW, GW).astype(jnp.bfloat16)
    b_s_full = jnp.broadcast_to(jnp.swapaxes(b_s_d, 1, 2)[:, :, :, None],
                                (DEPTH, SGU_CHUNK, N_HEADS_D, SGU_HEAD_DIM)).reshape(DEPTH, SGU_CHUNK, GW)
    vec_specs = [_const_spec((DEPTH, GW))] * 7

    T = ROW_TILE

    def row_tile(g):
        return (jnp.maximum(g - N_PREP, 0), 0)

    def weight_chunk(g):
        s = jnp.minimum(g, N_PREP - 1)
        return (s // PREP_CHUNKS, 0, s % PREP_CHUNKS)

    out = pl.pallas_call(
        _trunk_kernel,
        grid=(N_PREP + SEQ // T,),
        in_specs=[
            pl.BlockSpec((T, D_MODEL), row_tile),
            _const_spec((1, D_MODEL)),
            pl.BlockSpec((1, D_MODEL, PREP_IN_COLS), weight_chunk),
            _const_spec((DEPTH, 3 * D_MODEL)),
            _const_spec((DEPTH, D_MODEL)),
            _const_spec((1, D_MODEL)),
            pl.BlockSpec((1, D_MODEL, PREP_IN_COLS), weight_chunk),
            pl.BlockSpec((1, D_MODEL, PREP_OUT_COLS), weight_chunk),
            _const_spec((DEPTH, SHORT_CONV, GW)),
            _const_spec((DEPTH, CONF_WIDTH, GW)),
            *vec_specs,
            _const_spec((DEPTH, GW, GW)),
            _const_spec((DEPTH, GW, GW)),
            _const_spec((DEPTH, N_HEADS_D, SGU_CHUNK, SGU_CHUNK)),
            _const_spec((DEPTH, SGU_CHUNK, GW)),
        ],
        out_specs=pl.BlockSpec((T, D_MODEL), row_tile),
        out_shape=jax.ShapeDtypeStruct((SEQ, D_MODEL), f32),
        scratch_shapes=[
            pltpu.VMEM((DEPTH, N_CT, HEAD_A + T, LANES), f32),
            pltpu.VMEM((DEPTH, N_CT, HEAD_P + T, LANES), f32),
            pltpu.VMEM((DEPTH, N_CT, HEAD_P + T, LANES), f32),
            pltpu.VMEM((DEPTH, HEAD_P + T, LANES), f32),
            pltpu.VMEM((DEPTH, HEAD_P + T, LANES), f32),
            pltpu.VMEM((DEPTH, N_CT, HEAD_C + T, LANES), f32),
            pltpu.VMEM((DEPTH, SUBLANES, 3 * D_MODEL), f32),
            pltpu.VMEM((DEPTH, D_MODEL, D_IN), jnp.bfloat16),
            pltpu.VMEM((DEPTH, D_MODEL, D_MODEL), jnp.bfloat16),
            pltpu.VMEM((DEPTH, SGU_CHUNK, N_HEADS_D * SGU_CHUNK), jnp.bfloat16),
            pltpu.VMEM((DEPTH, GW, GW), jnp.bfloat16),
        ],
        compiler_params=pltpu.CompilerParams(
            dimension_semantics=("arbitrary",),
            vmem_limit_bytes=VMEM_LIMIT_BYTES),
        name="hybrid_trunk",
    )(x.reshape(SEQ, D_MODEL), c, w_ada, b_ada, norm_g, final_g.reshape(1, D_MODEL),
      w_in, w_out, w_conv_a, w_dw_c,
      pool_scale, b_dw_c, ln_g_c, ln_b_c, b_pw2_c, ln_g_d, ln_b_d,
      w_pool_bd, w_pw2_c, w_s_d, b_s_full)
    return out.reshape(1, SEQ, D_MODEL)
```

```python
import math

import jax
import jax.numpy as jnp
from jax import lax
from jax.experimental import pallas as pl
from jax.experimental.pallas import tpu as pltpu

D_MODEL = 1024
SEQ = 16384
DEPTH = 2
GW = 256
D_IN = 12 * GW
SHORT_CONV = 3
CONF_WIDTH = 31
POOL_WINDOWS = (2, 4, 8, 16)
POOL_GROUP_DIM = GW // len(POOL_WINDOWS)
SGU_CHUNK = 128
N_HEADS_D = 4
SGU_HEAD_DIM = GW // N_HEADS_D
EPS = 1e-6

SUBLANES = 8
LANES = 128
N_CT = GW // LANES
ROW_TILE = 512
HEAD_A = 8
HEAD_P = 8
HEAD_C = 32
VMEM_LIMIT_BYTES = 56 * 1024 * 1024
HALVED_GROUPS = (3, 5, 7, 8, 11)
PREP_CHUNKS = 4
N_PREP = DEPTH * PREP_CHUNKS
PREP_IN_COLS = D_IN // PREP_CHUNKS
PREP_OUT_COLS = D_MODEL // PREP_CHUNKS
Z_GROUP_ORDER = (6, 7, 1, 2, 0, 3, 10, 9, 11, 4, 5, 8)


def _silu(x):
    hx = 0.5 * x
    return hx + hx * jnp.tanh(hx)


def _silu_of_half(hx):
    return hx + hx * jnp.tanh(hx)


def _sigmoid_of_half(hx):
    return 0.5 + 0.5 * jnp.tanh(hx)


def _gelu_tanh(x):
    c = math.sqrt(2.0 / math.pi)
    hx = 0.5 * x
    return hx + hx * jnp.tanh(x * ((x * x) * (0.044715 * c) + c))


def _layer_norm(x, g, b):
    mu = jnp.mean(x, axis=-1, keepdims=True)
    xc = x - mu
    var = jnp.mean(xc * xc, axis=-1, keepdims=True)
    return xc * lax.rsqrt(var + EPS) * g + b


def _rms_norm(x, g):
    ms = jnp.mean(x * x, axis=-1, keepdims=True)
    return (x * lax.rsqrt(ms + EPS)) * g


def _trunk_kernel(x_ref, c_ref, w_ada_ref, b_ada_ref, norm_g_ref, final_g_ref, w_in_ref, w_out_ref,
                  w_conv_a_ref, w_dw_c_ref, pool_scale_ref, b_dw_c_ref, ln_g_c_ref, ln_b_c_ref, b_pw2_c_ref,
                  ln_g_d_ref, ln_b_d_ref, w_pool_ref, w_pw2_ref, w_s_ref, b_s_ref, o_ref,
                  cx_buf, p_buf, s2_buf, s4_buf, s8_buf, h_buf, mod_buf, w_in_buf, w_out_buf, w_s_buf, w_pw2_buf):
    g = pl.program_id(0)
    T = ROW_TILE

    for s in range(N_PREP):
        l, j = divmod(s, PREP_CHUNKS)

        @pl.when(g == s)
        def _(l=l, j=j):
            in_cols = slice(j * PREP_IN_COLS, (j + 1) * PREP_IN_COLS)
            out_cols = slice(j * PREP_OUT_COLS, (j + 1) * PREP_OUT_COLS)
            for k in range(PREP_IN_COLS // GW):
                grp = j * PREP_IN_COLS // GW + k
                stored = Z_GROUP_ORDER.index(grp)
                w_grp = w_in_ref[0, :, k * GW:(k + 1) * GW]
                if grp in HALVED_GROUPS:
                    w_grp = w_grp * 0.5
                w_in_buf[l, :, stored * GW:(stored + 1) * GW] = w_grp.astype(jnp.bfloat16)
            w_out_buf[l, :, out_cols] = w_out_ref[0].astype(jnp.bfloat16)
            if j == 0:
                tri = (lax.broadcasted_iota(jnp.int32, (SGU_CHUNK, SGU_CHUNK), 0)
                       >= lax.broadcasted_iota(jnp.int32, (SGU_CHUNK, SGU_CHUNK), 1))
                w_pw2_buf[l] = w_pw2_ref[l].astype(jnp.bfloat16)
                w_s_buf[l] = jnp.concatenate([jnp.where(tri, w_s_ref[l, hd], 0.0).astype(jnp.bfloat16)
                                              for hd in range(N_HEADS_D)], axis=1)
            ca = _silu(jnp.broadcast_to(c_ref[...], (SUBLANES, D_MODEL))).astype(jnp.bfloat16)
            mod_buf[l, :, in_cols] = (jnp.dot(ca, w_ada_ref[0].astype(jnp.bfloat16),
                                              preferred_element_type=jnp.float32)
                                      + b_ada_ref[l:l + 1, in_cols])

    @pl.when(g == 0)
    def _():
        cx_buf[:, :, 0:HEAD_A, :] = jnp.zeros((DEPTH, N_CT, HEAD_A, LANES), jnp.float32)
        for buf in (p_buf, s2_buf):
            buf[:, :, 0:HEAD_P, :] = jnp.zeros((DEPTH, N_CT, HEAD_P, LANES), jnp.float32)
        for buf in (s4_buf, s8_buf):
            buf[:, 0:HEAD_P, :] = jnp.zeros((DEPTH, HEAD_P, LANES), jnp.float32)
        h_buf[:, :, 0:HEAD_C, :] = jnp.zeros((DEPTH, N_CT, HEAD_C, LANES), jnp.float32)

    @pl.when(g >= N_PREP)
    def _():
        _trunk_step(g - N_PREP, x_ref, mod_buf, norm_g_ref, final_g_ref, w_in_buf, w_out_buf,
                    w_conv_a_ref, w_dw_c_ref, pool_scale_ref, b_dw_c_ref, ln_g_c_ref, ln_b_c_ref, b_pw2_c_ref,
                    ln_g_d_ref, ln_b_d_ref, w_pool_ref, w_pw2_buf, w_s_buf, b_s_ref, o_ref,
                    cx_buf, p_buf, s2_buf, s4_buf, s8_buf, h_buf)


def _trunk_step(i, x_ref, mod_ref, norm_g_ref, final_g_ref, w_in_ref, w_out_ref,
                w_conv_a_ref, w_dw_c_ref, pool_scale_ref, b_dw_c_ref, ln_g_c_ref, ln_b_c_ref, b_pw2_c_ref,
                ln_g_d_ref, ln_b_d_ref, w_pool_ref, w_pw2_ref, w_s_ref, b_s_ref, o_ref,
                cx_buf, p_buf, s2_buf, s4_buf, s8_buf, h_buf):
    T = ROW_TILE
    lane = lax.broadcasted_iota(jnp.int32, (T, LANES), 1)
    left = lane < POOL_GROUP_DIM
    top = max(POOL_WINDOWS)
    row1 = lax.broadcasted_iota(jnp.int32, (top, LANES), 0) + (i * T + 1)
    inv_count = []
    left_top = lax.broadcasted_iota(jnp.int32, (top, LANES), 1) < POOL_GROUP_DIM
    left_row = lax.broadcasted_iota(jnp.int32, (1, LANES), 1) < POOL_GROUP_DIM
    for ct in range(N_CT):
        w_left, w_right = POOL_WINDOWS[2 * ct], POOL_WINDOWS[2 * ct + 1]
        window = jnp.where(left_top, w_left, w_right)
        inv_count.append((1.0 / jnp.minimum(row1, window).astype(jnp.float32),
                          jnp.where(left_row, 1.0 / w_left, 1.0 / w_right)))
    head_of_lane = lax.broadcasted_iota(jnp.int32, (SGU_CHUNK, GW), 1) // SGU_HEAD_DIM

    def vec(l, ref):
        return ref[l:l + 1, :]

    def shifted(buf, idx, head, back):
        return buf[idx + (slice(head - back, head - back + T), slice(None))]

    def carry(buf, idx, head):
        buf[idx + (slice(0, head), slice(None))] = buf[idx + (slice(T, T + head), slice(None))]

    def layer(l, x_cur):
        shift = mod_ref[l, 0:1, 0:D_MODEL]
        scale = mod_ref[l, 0:1, D_MODEL:2 * D_MODEL]
        gate = mod_ref[l, 0:1, 2 * D_MODEL:3 * D_MODEL]
        h = (_rms_norm(x_cur, norm_g_ref[l:l + 1, :] * (1.0 + scale)) + shift).astype(jnp.bfloat16)

        def short_conv(a_b, a_c, a_x, a_gh):
            cx = a_c * a_x
            convs = []
            for ct in range(N_CT):
                lanes = slice(ct * LANES, (ct + 1) * LANES)
                cx_buf[l, ct, HEAD_A:HEAD_A + T, :] = cx[:, lanes]
                conv = None
                for j in range(SHORT_CONV):
                    term = (w_conv_a_ref[l, j:j + 1, lanes]
                            * shifted(cx_buf, (l, ct), HEAD_A, SHORT_CONV - 1 - j))
                    conv = term if conv is None else conv + term
                carry(cx_buf, (l, ct), HEAD_A)
                convs.append(conv)
            return (a_b * jnp.concatenate(convs, axis=-1)) * _silu_of_half(a_gh)

        def pooling(b_p, b_gh):
            pooled = []
            for ct in range(N_CT):
                p = b_p[:, ct * LANES:(ct + 1) * LANES]
                p_buf[l, ct, HEAD_P:HEAD_P + T, :] = p
                s2 = p + shifted(p_buf, (l, ct), HEAD_P, 1)
                s2_buf[l, ct, HEAD_P:HEAD_P + T, :] = s2
                s4 = s2 + shifted(s2_buf, (l, ct), HEAD_P, 2)
                carry(p_buf, (l, ct), HEAD_P)
                carry(s2_buf, (l, ct), HEAD_P)
                if ct == 0:
                    wsum = jnp.where(left, s2, s4)
                else:
                    s4_buf[l, HEAD_P:HEAD_P + T, :] = s4
                    s8 = s4 + shifted(s4_buf, (l,), HEAD_P, 4)
                    s8_buf[l, HEAD_P:HEAD_P + T, :] = s8
                    s16 = s8 + shifted(s8_buf, (l,), HEAD_P, 8)
                    carry(s4_buf, (l,), HEAD_P)
                    carry(s8_buf, (l,), HEAD_P)
                    wsum = jnp.where(left, s8, s16)
                inv_top, inv_rest = inv_count[ct]
                pooled.append(jnp.concatenate([wsum[0:top] * inv_top - p[0:top],
                                               wsum[top:] * inv_rest - p[top:]], axis=0))
            pooled = jnp.concatenate(pooled, axis=-1)
            y_b = jnp.dot(pooled.astype(jnp.bfloat16), w_pool_ref[l], preferred_element_type=jnp.float32)
            return (y_b * vec(l, pool_scale_ref)) * _silu_of_half(b_gh)

        def conformer(c_a, c_glh, c_gh):
            hgl = c_a * _sigmoid_of_half(c_glh)
            accs = []
            for ct in range(N_CT):
                lanes = slice(ct * LANES, (ct + 1) * LANES)
                h_buf[l, ct, HEAD_C:HEAD_C + T, :] = hgl[:, lanes]
                acc = None
                for j in range(CONF_WIDTH):
                    term = (w_dw_c_ref[l, j:j + 1, lanes]
                            * shifted(h_buf, (l, ct), HEAD_C, CONF_WIDTH - 1 - j))
                    acc = term if acc is None else acc + term
                carry(h_buf, (l, ct), HEAD_C)
                accs.append(acc)
            hc = jnp.concatenate(accs, axis=-1) + vec(l, b_dw_c_ref)
            hc = _silu(_layer_norm(hc, vec(l, ln_g_c_ref), vec(l, ln_b_c_ref)))
            y_c = jnp.dot(hc.astype(jnp.bfloat16), w_pw2_ref[l], preferred_element_type=jnp.float32)
            return (y_c + vec(l, b_pw2_c_ref)) * _silu_of_half(c_gh)

        def sgu(d_u, d_v, d_gh):
            u = _gelu_tanh(d_u)
            v = _layer_norm(_gelu_tanh(d_v), vec(l, ln_g_d_ref), vec(l, ln_b_d_ref))
            mixed_chunks = []
            for ck in range(T // SGU_CHUNK):
                v_ck = v[ck * SGU_CHUNK:(ck + 1) * SGU_CHUNK, :]
                stacked = jnp.concatenate([jnp.where(head_of_lane == hd, v_ck, 0.0).astype(jnp.bfloat16)
                                           for hd in range(N_HEADS_D)], axis=0)
                mixed_chunks.append(jnp.dot(w_s_ref[l], stacked, preferred_element_type=jnp.float32)
                                    + b_s_ref[l])
            mixed = jnp.concatenate(mixed_chunks, axis=0)
            return (u * mixed) * _silu_of_half(d_gh)

        z = jnp.dot(h, w_in_ref[l], preferred_element_type=jnp.float32)
        zs = [z[:, s * GW:(s + 1) * GW] for s in (Z_GROUP_ORDER.index(k) for k in range(D_IN // GW))]
        y_c = conformer(*zs[6:9])
        y_d = sgu(*zs[9:12])
        y_a = short_conv(*zs[0:4])
        y_b = pooling(*zs[4:6])
        y_cat = jnp.concatenate([y_a, y_b, y_c, y_d], axis=-1).astype(jnp.bfloat16)
        y = jnp.dot(y_cat, w_out_ref[l], preferred_element_type=jnp.float32)
        return x_cur + gate * y

    x_cur = x_ref[...]
    for l in range(DEPTH):
        x_cur = layer(l, x_cur)
    o_ref[...] = _rms_norm(x_cur, final_g_ref[...])


def _const_spec(shape):
    zeros = (0,) * len(shape)
    return pl.BlockSpec(shape, lambda i: zeros, pipeline_mode=pl.Buffered(1))


def kernel(x, c, norm_g, w_ada, b_ada, w_in, w_conv_a, w_pool, pool_scale, w_dw_c, b_dw_c, ln_g_c, ln_b_c, w_pw2_c, b_pw2_c, ln_g_d, ln_b_d, w_s_d, b_s_d, w_out, final_g):
    f32 = jnp.float32
    assert x.shape == (1, SEQ, D_MODEL) and c.shape == (1, D_MODEL)
    assert POOL_WINDOWS == (2, 4, 8, 16) and N_CT == 2

    n_grp = len(POOL_WINDOWS)
    w_pool_bd = (w_pool[:, :, :, None, :] * jnp.eye(n_grp, dtype=f32)[None, :, None, :, None]
                 ).reshape(DEPTH, GW, GW).astype(jnp.bfloat16)
    b_s_full = jnp.broadcast_to(jnp.swapaxes(b_s_d, 1, 2)[:, :, :, None],
                                (DEPTH, SGU_CHUNK, N_HEADS_D, SGU_HEAD_DIM)).reshape(DEPTH, SGU_CHUNK, GW)
    vec_specs = [_const_spec((DEPTH, GW))] * 7

    T = ROW_TILE

    def row_tile(g):
        return (jnp.maximum(g - N_PREP, 0), 0)

    def weight_chunk(g):
        s = jnp.minimum(g, N_PREP - 1)
        return (s // PREP_CHUNKS, 0, s % PREP_CHUNKS)

    out = pl.pallas_call(
        _trunk_kernel,
        grid=(N_PREP + SEQ // T,),
        in_specs=[
            pl.BlockSpec((T, D_MODEL), row_tile),
            _const_spec((1, D_MODEL)),
            pl.BlockSpec((1, D_MODEL, PREP_IN_COLS), weight_chunk),
            _const_spec((DEPTH, 3 * D_MODEL)),
            _const_spec((DEPTH, D_MODEL)),
            _const_spec((1, D_MODEL)),
            pl.BlockSpec((1, D_MODEL, PREP_IN_COLS), weight_chunk),
            pl.BlockSpec((1, D_MODEL, PREP_OUT_COLS), weight_chunk),
            _const_spec((DEPTH, SHORT_CONV, GW)),
            _const_spec((DEPTH, CONF_WIDTH, GW)),
            *vec_specs,
            _const_spec((DEPTH, GW, GW)),
            _const_spec((DEPTH, GW, GW)),
            _const_spec((DEPTH, N_HEADS_D, SGU_CHUNK, SGU_CHUNK)),
            _const_spec((DEPTH, SGU_CHUNK, GW)),
        ],
        out_specs=pl.BlockSpec((T, D_MODEL), row_tile),
        out_shape=jax.ShapeDtypeStruct((SEQ, D_MODEL), f32),
        scratch_shapes=[
            pltpu.VMEM((DEPTH, N_CT, HEAD_A + T, LANES), f32),
            pltpu.VMEM((DEPTH, N_CT, HEAD_P + T, LANES), f32),
            pltpu.VMEM((DEPTH, N_CT, HEAD_P + T, LANES), f32),
            pltpu.VMEM((DEPTH, HEAD_P + T, LANES), f32),
            pltpu.VMEM((DEPTH, HEAD_P + T, LANES), f32),
            pltpu.VMEM((DEPTH, N_CT, HEAD_C + T, LANES), f32),
            pltpu.VMEM((DEPTH, SUBLANES, 3 * D_MODEL), f32),
            pltpu.VMEM((DEPTH, D_MODEL, D_IN), jnp.bfloat16),
            pltpu.VMEM((DEPTH, D_MODEL, D_MODEL), jnp.bfloat16),
            pltpu.VMEM((DEPTH, SGU_CHUNK, N_HEADS_D * SGU_CHUNK), jnp.bfloat16),
            pltpu.VMEM((DEPTH, GW, GW), jnp.bfloat16),
        ],
        compiler_params=pltpu.CompilerParams(
            dimension_semantics=("arbitrary",),
            vmem_limit_bytes=VMEM_LIMIT_BYTES),
        name="hybrid_trunk",
    )(x.reshape(SEQ, D_MODEL), c, w_ada, b_ada, norm_g, final_g.reshape(1, D_MODEL),
      w_in, w_out, w_conv_a, w_dw_c,
      pool_scale, b_dw_c, ln_g_c, ln_b_c, b_pw2_c, ln_g_d, ln_b_d,
      w_pool_bd, w_pw2_c, w_s_d, b_s_full)
    return out.reshape(1, SEQ, D_MODEL)
```

```python
import math

import jax
import jax.numpy as jnp
from jax import lax
from jax.experimental import pallas as pl
from jax.experimental.pallas import tpu as pltpu

D_MODEL = 1024
SEQ = 16384
DEPTH = 2
GW = 256
D_IN = 12 * GW
SHORT_CONV = 3
CONF_WIDTH = 31
POOL_WINDOWS = (2, 4, 8, 16)
POOL_GROUP_DIM = GW // len(POOL_WINDOWS)
SGU_CHUNK = 128
N_HEADS_D = 4
SGU_HEAD_DIM = GW // N_HEADS_D
EPS = 1e-6

SUBLANES = 8
LANES = 128
N_CT = GW // LANES
ROW_TILE = 512
HEAD_A = 8
HEAD_P = 8
HEAD_C = 32
VMEM_LIMIT_BYTES = 56 * 1024 * 1024
HALVED_GROUPS = (3, 5, 7, 8, 11)
PREP_CHUNKS = 4
N_PREP = DEPTH * PREP_CHUNKS
PREP_IN_COLS = D_IN // PREP_CHUNKS
PREP_OUT_COLS = D_MODEL // PREP_CHUNKS
Z_GROUP_ORDER = (6, 7, 10, 9, 11, 1, 2, 0, 3, 4, 5, 8)


def _silu(x):
    hx = 0.5 * x
    return hx + hx * jnp.tanh(hx)


def _silu_of_half(hx):
    return hx + hx * jnp.tanh(hx)


def _sigmoid_of_half(hx):
    return 0.5 + 0.5 * jnp.tanh(hx)


def _gelu_tanh(x):
    c = math.sqrt(2.0 / math.pi)
    hx = 0.5 * x
    return hx + hx * jnp.tanh(x * ((x * x) * (0.044715 * c) + c))


def _layer_norm(x, g, b):
    mu = jnp.mean(x, axis=-1, keepdims=True)
    xc = x - mu
    var = jnp.mean(xc * xc, axis=-1, keepdims=True)
    return xc * lax.rsqrt(var + EPS) * g + b


def _rms_norm(x, g):
    ms = jnp.mean(x * x, axis=-1, keepdims=True)
    return (x * lax.rsqrt(ms + EPS)) * g


def _trunk_kernel(x_ref, c_ref, w_ada_ref, b_ada_ref, norm_g_ref, final_g_ref, w_in_ref, w_out_ref,
                  w_conv_a_ref, w_dw_c_ref, pool_scale_ref, b_dw_c_ref, ln_g_c_ref, ln_b_c_ref, b_pw2_c_ref,
                  ln_g_d_ref, ln_b_d_ref, w_pool_ref, w_pw2_ref, w_s_ref, b_s_ref, o_ref,
                  cx_buf, p_buf, s2_buf, s4_buf, s8_buf, h_buf, mod_buf, w_in_buf, w_out_buf, w_s_buf, w_pw2_buf):
    g = pl.program_id(0)
    T = ROW_TILE

    for s in range(N_PREP):
        l, j = divmod(s, PREP_CHUNKS)

        @pl.when(g == s)
        def _(l=l, j=j):
            in_cols = slice(j * PREP_IN_COLS, (j + 1) * PREP_IN_COLS)
            out_cols = slice(j * PREP_OUT_COLS, (j + 1) * PREP_OUT_COLS)
            for k in range(PREP_IN_COLS // GW):
                grp = j * PREP_IN_COLS // GW + k
                stored = Z_GROUP_ORDER.index(grp)
                w_grp = w_in_ref[0, :, k * GW:(k + 1) * GW]
                if grp in HALVED_GROUPS:
                    w_grp = w_grp * 0.5
                w_in_buf[l, :, stored * GW:(stored + 1) * GW] = w_grp.astype(jnp.bfloat16)
            w_out_buf[l, :, out_cols] = w_out_ref[0].astype(jnp.bfloat16)
            if j == 0:
                tri = (lax.broadcasted_iota(jnp.int32, (SGU_CHUNK, SGU_CHUNK), 0)
                       >= lax.broadcasted_iota(jnp.int32, (SGU_CHUNK, SGU_CHUNK), 1))
                w_pw2_buf[l] = w_pw2_ref[l].astype(jnp.bfloat16)
                w_s_buf[l] = jnp.concatenate([jnp.where(tri, w_s_ref[l, hd], 0.0).astype(jnp.bfloat16)
                                              for hd in range(N_HEADS_D)], axis=1)
            ca = _silu(jnp.broadcast_to(c_ref[...], (SUBLANES, D_MODEL))).astype(jnp.bfloat16)
            mod_buf[l, :, in_cols] = (jnp.dot(ca, w_ada_ref[0].astype(jnp.bfloat16),
                                              preferred_element_type=jnp.float32)
                                      + b_ada_ref[l:l + 1, in_cols])

    @pl.when(g == 0)
    def _():
        cx_buf[:, :, 0:HEAD_A, :] = jnp.zeros((DEPTH, N_CT, HEAD_A, LANES), jnp.float32)
        for buf in (p_buf, s2_buf):
            buf[:, :, 0:HEAD_P, :] = jnp.zeros((DEPTH, N_CT, HEAD_P, LANES), jnp.float32)
        for buf in (s4_buf, s8_buf):
            buf[:, 0:HEAD_P, :] = jnp.zeros((DEPTH, HEAD_P, LANES), jnp.float32)
        h_buf[:, :, 0:HEAD_C, :] = jnp.zeros((DEPTH, N_CT, HEAD_C, LANES), jnp.float32)

    @pl.when(g >= N_PREP)
    def _():
        _trunk_step(g - N_PREP, x_ref, mod_buf, norm_g_ref, final_g_ref, w_in_buf, w_out_buf,
                    w_conv_a_ref, w_dw_c_ref, pool_scale_ref, b_dw_c_ref, ln_g_c_ref, ln_b_c_ref, b_pw2_c_ref,
                    ln_g_d_ref, ln_b_d_ref, w_pool_ref, w_pw2_buf, w_s_buf, b_s_ref, o_ref,
                    cx_buf, p_buf, s2_buf, s4_buf, s8_buf, h_buf)


def _trunk_step(i, x_ref, mod_ref, norm_g_ref, final_g_ref, w_in_ref, w_out_ref,
                w_conv_a_ref, w_dw_c_ref, pool_scale_ref, b_dw_c_ref, ln_g_c_ref, ln_b_c_ref, b_pw2_c_ref,
                ln_g_d_ref, ln_b_d_ref, w_pool_ref, w_pw2_ref, w_s_ref, b_s_ref, o_ref,
                cx_buf, p_buf, s2_buf, s4_buf, s8_buf, h_buf):
    T = ROW_TILE
    lane = lax.broadcasted_iota(jnp.int32, (T, LANES), 1)
    left = lane < POOL_GROUP_DIM
    top = max(POOL_WINDOWS)
    row1 = lax.broadcasted_iota(jnp.int32, (top, LANES), 0) + (i * T + 1)
    inv_count = []
    left_top = lax.broadcasted_iota(jnp.int32, (top, LANES), 1) < POOL_GROUP_DIM
    left_row = lax.broadcasted_iota(jnp.int32, (1, LANES), 1) < POOL_GROUP_DIM
    for ct in range(N_CT):
        w_left, w_right = POOL_WINDOWS[2 * ct], POOL_WINDOWS[2 * ct + 1]
        window = jnp.where(left_top, w_left, w_right)
        inv_count.append((1.0 / jnp.minimum(row1, window).astype(jnp.float32),
                          jnp.where(left_row, 1.0 / w_left, 1.0 / w_right)))
    head_of_lane = lax.broadcasted_iota(jnp.int32, (SGU_CHUNK, GW), 1) // SGU_HEAD_DIM

    def vec(l, ref):
        return ref[l:l + 1, :]

    def shifted(buf, idx, head, back):
        return buf[idx + (slice(head - back, head - back + T), slice(None))]

    def carry(buf, idx, head):
        buf[idx + (slice(0, head), slice(None))] = buf[idx + (slice(T, T + head), slice(None))]

    def layer(l, x_cur):
        shift = mod_ref[l, 0:1, 0:D_MODEL]
        scale = mod_ref[l, 0:1, D_MODEL:2 * D_MODEL]
        gate = mod_ref[l, 0:1, 2 * D_MODEL:3 * D_MODEL]
        h = (_rms_norm(x_cur, norm_g_ref[l:l + 1, :] * (1.0 + scale)) + shift).astype(jnp.bfloat16)

        def short_conv(a_b, a_c, a_x, a_gh):
            cx = a_c * a_x
            convs = []
            for ct in range(N_CT):
                lanes = slice(ct * LANES, (ct + 1) * LANES)
                cx_buf[l, ct, HEAD_A:HEAD_A + T, :] = cx[:, lanes]
                conv = None
                for j in range(SHORT_CONV):
                    term = (w_conv_a_ref[l, j:j + 1, lanes]
                            * shifted(cx_buf, (l, ct), HEAD_A, SHORT_CONV - 1 - j))
                    conv = term if conv is None else conv + term
                carry(cx_buf, (l, ct), HEAD_A)
                convs.append(conv)
            return (a_b * jnp.concatenate(convs, axis=-1)) * _silu_of_half(a_gh)

        def pooling(b_p, b_gh):
            pooled = []
            for ct in range(N_CT):
                p = b_p[:, ct * LANES:(ct + 1) * LANES]
                p_buf[l, ct, HEAD_P:HEAD_P + T, :] = p
                s2 = p + shifted(p_buf, (l, ct), HEAD_P, 1)
                s2_buf[l, ct, HEAD_P:HEAD_P + T, :] = s2
                s4 = s2 + shifted(s2_buf, (l, ct), HEAD_P, 2)
                carry(p_buf, (l, ct), HEAD_P)
                carry(s2_buf, (l, ct), HEAD_P)
                if ct == 0:
                    wsum = jnp.where(left, s2, s4)
                else:
                    s4_buf[l, HEAD_P:HEAD_P + T, :] = s4
                    s8 = s4 + shifted(s4_buf, (l,), HEAD_P, 4)
                    s8_buf[l, HEAD_P:HEAD_P + T, :] = s8
                    s16 = s8 + shifted(s8_buf, (l,), HEAD_P, 8)
                    carry(s4_buf, (l,), HEAD_P)
                    carry(s8_buf, (l,), HEAD_P)
                    wsum = jnp.where(left, s8, s16)
                inv_top, inv_rest = inv_count[ct]
                pooled.append(jnp.concatenate([wsum[0:top] * inv_top - p[0:top],
                                               wsum[top:] * inv_rest - p[top:]], axis=0))
            pooled = jnp.concatenate(pooled, axis=-1)
            y_b = jnp.dot(pooled.astype(jnp.bfloat16), w_pool_ref[l], preferred_element_type=jnp.float32)
            return (y_b * vec(l, pool_scale_ref)) * _silu_of_half(b_gh)

        def conformer(c_a, c_glh, c_gh):
            hgl = c_a * _sigmoid_of_half(c_glh)
            accs = []
            for ct in range(N_CT):
                lanes = slice(ct * LANES, (ct + 1) * LANES)
                h_buf[l, ct, HEAD_C:HEAD_C + T, :] = hgl[:, lanes]
                acc = None
                for j in range(CONF_WIDTH):
                    term = (w_dw_c_ref[l, j:j + 1, lanes]
                            * shifted(h_buf, (l, ct), HEAD_C, CONF_WIDTH - 1 - j))
                    acc = term if acc is None else acc + term
                carry(h_buf, (l, ct), HEAD_C)
                accs.append(acc)
            hc = jnp.concatenate(accs, axis=-1) + vec(l, b_dw_c_ref)
            hc = _silu(_layer_norm(hc, vec(l, ln_g_c_ref), vec(l, ln_b_c_ref)))
            y_c = jnp.dot(hc.astype(jnp.bfloat16), w_pw2_ref[l], preferred_element_type=jnp.float32)
            return (y_c + vec(l, b_pw2_c_ref)) * _silu_of_half(c_gh)

        def sgu(d_u, d_v, d_gh):
            u = _gelu_tanh(d_u)
            v = _layer_norm(_gelu_tanh(d_v), vec(l, ln_g_d_ref), vec(l, ln_b_d_ref))
            mixed_chunks = []
            for ck in range(T // SGU_CHUNK):
                v_ck = v[ck * SGU_CHUNK:(ck + 1) * SGU_CHUNK, :]
                stacked = jnp.concatenate([jnp.where(head_of_lane == hd, v_ck, 0.0).astype(jnp.bfloat16)
                                           for hd in range(N_HEADS_D)], axis=0)
                mixed_chunks.append(jnp.dot(w_s_ref[l], stacked, preferred_element_type=jnp.float32)
                                    + b_s_ref[l])
            mixed = jnp.concatenate(mixed_chunks, axis=0)
            return (u * mixed) * _silu_of_half(d_gh)

        z = jnp.dot(h, w_in_ref[l], preferred_element_type=jnp.float32)
        zs = [z[:, s * GW:(s + 1) * GW] for s in (Z_GROUP_ORDER.index(k) for k in range(D_IN // GW))]
        y_c = conformer(*zs[6:9])
        y_d = sgu(*zs[9:12])
        y_a = short_conv(*zs[0:4])
        y_b = pooling(*zs[4:6])
        y_cat = jnp.concatenate([y_a, y_b, y_c, y_d], axis=-1).astype(jnp.bfloat16)
        y = jnp.dot(y_cat, w_out_ref[l], preferred_element_type=jnp.float32)
        return x_cur + gate * y

    x_cur = x_ref[...]
    for l in range(DEPTH):
        x_cur = layer(l, x_cur)
    o_ref[...] = _rms_norm(x_cur, final_g_ref[...])


def _const_spec(shape):
    zeros = (0,) * len(shape)
    return pl.BlockSpec(shape, lambda i: zeros, pipeline_mode=pl.Buffered(1))


def kernel(x, c, norm_g, w_ada, b_ada, w_in, w_conv_a, w_pool, pool_scale, w_dw_c, b_dw_c, ln_g_c, ln_b_c, w_pw2_c, b_pw2_c, ln_g_d, ln_b_d, w_s_d, b_s_d, w_out, final_g):
    f32 = jnp.float32
    assert x.shape == (1, SEQ, D_MODEL) and c.shape == (1, D_MODEL)
    assert POOL_WINDOWS == (2, 4, 8, 16) and N_CT == 2

    n_grp = len(POOL_WINDOWS)
    w_pool_bd = (w_pool[:, :, :, None, :] * jnp.eye(n_grp, dtype=f32)[None, :, None, :, None]
                 ).reshape(DEPTH, GW, GW).astype(jnp.bfloat16)
    b_s_full = jnp.broadcast_to(jnp.swapaxes(b_s_d, 1, 2)[:, :, :, None],
                                (DEPTH, SGU_CHUNK, N_HEADS_D, SGU_HEAD_DIM)).reshape(DEPTH, SGU_CHUNK, GW)
    vec_specs = [_const_spec((DEPTH, GW))] * 7

    T = ROW_TILE

    def row_tile(g):
        return (jnp.maximum(g - N_PREP, 0), 0)

    def weight_chunk(g):
        s = jnp.minimum(g, N_PREP - 1)
        return (s // PREP_CHUNKS, 0, s % PREP_CHUNKS)

    out = pl.pallas_call(
        _trunk_kernel,
        grid=(N_PREP + SEQ // T,),
        in_specs=[
            pl.BlockSpec((T, D_MODEL), row_tile),
            _const_spec((1, D_MODEL)),
            pl.BlockSpec((1, D_MODEL, PREP_IN_COLS), weight_chunk),
            _const_spec((DEPTH, 3 * D_MODEL)),
            _const_spec((DEPTH, D_MODEL)),
            _const_spec((1, D_MODEL)),
            pl.BlockSpec((1, D_MODEL, PREP_IN_COLS), weight_chunk),
            pl.BlockSpec((1, D_MODEL, PREP_OUT_COLS), weight_chunk),
            _const_spec((DEPTH, SHORT_CONV, GW)),
            _const_spec((DEPTH, CONF_WIDTH, GW)),
            *vec_specs,
            _const_spec((DEPTH, GW, GW)),
            _const_spec((DEPTH, GW, GW)),
            _const_spec((DEPTH, N_HEADS_D, SGU_CHUNK, SGU_CHUNK)),
            _const_spec((DEPTH, SGU_CHUNK, GW)),
        ],
        out_specs=pl.BlockSpec((T, D_MODEL), row_tile),
        out_shape=jax.ShapeDtypeStruct((SEQ, D_MODEL), f32),
        scratch_shapes=[
            pltpu.VMEM((DEPTH, N_CT, HEAD_A + T, LANES), f32),
            pltpu.VMEM((DEPTH, N_CT, HEAD_P + T, LANES), f32),
            pltpu.VMEM((DEPTH, N_CT, HEAD_P + T, LANES), f32),
            pltpu.VMEM((DEPTH, HEAD_P + T, LANES), f32),
            pltpu.VMEM((DEPTH, HEAD_P + T, LANES), f32),
            pltpu.VMEM((DEPTH, N_CT, HEAD_C + T, LANES), f32),
            pltpu.VMEM((DEPTH, SUBLANES, 3 * D_MODEL), f32),
            pltpu.VMEM((DEPTH, D_MODEL, D_IN), jnp.bfloat16),
            pltpu.VMEM((DEPTH, D_MODEL, D_MODEL), jnp.bfloat16),
            pltpu.VMEM((DEPTH, SGU_CHUNK, N_HEADS_D * SGU_CHUNK), jnp.bfloat16),
            pltpu.VMEM((DEPTH, GW, GW), jnp.bfloat16),
        ],
        compiler_params=pltpu.CompilerParams(
            dimension_semantics=("arbitrary",),
            vmem_limit_bytes=VMEM_LIMIT_BYTES),
        name="hybrid_trunk",
    )(x.reshape(SEQ, D_MODEL), c, w_ada, b_ada, norm_g, final_g.reshape(1, D_MODEL),
      w_in, w_out, w_conv_a, w_dw_c,
      pool_scale, b_dw_c, ln_g_c, ln_b_c, b_pw2_c, ln_g_d, ln_b_d,
      w_pool_bd, w_pw2_c, w_s_d, b_s_full)
    return out.reshape(1, SEQ, D_MODEL)
```

```python
import math

import jax
import jax.numpy as jnp
from jax import lax
from jax.experimental import pallas as pl
from jax.experimental.pallas import tpu as pltpu

D_MODEL = 1024
SEQ = 16384
DEPTH = 2
GW = 256
D_IN = 12 * GW
SHORT_CONV = 3
CONF_WIDTH = 31
POOL_WINDOWS = (2, 4, 8, 16)
POOL_GROUP_DIM = GW // len(POOL_WINDOWS)
SGU_CHUNK = 128
N_HEADS_D = 4
SGU_HEAD_DIM = GW // N_HEADS_D
EPS = 1e-6

SUBLANES = 8
LANES = 128
N_CT = GW // LANES
ROW_TILE = 512
HEAD_A = 8
HEAD_P = 8
HEAD_C = 32
VMEM_LIMIT_BYTES = 56 * 1024 * 1024
HALVED_GROUPS = (3, 5, 7, 8, 11)
PREP_CHUNKS = 4
N_PREP = DEPTH * PREP_CHUNKS
PREP_IN_COLS = D_IN // PREP_CHUNKS
PREP_OUT_COLS = D_MODEL // PREP_CHUNKS
Z_GROUP_ORDER = (6, 7, 1, 2, 0, 3, 10, 9, 11, 4, 5, 8)


def _silu(x):
    hx = 0.5 * x
    return hx + hx * jnp.tanh(hx)


def _silu_of_half(hx):
    return hx + hx * jnp.tanh(hx)


def _sigmoid_of_half(hx):
    return 0.5 + 0.5 * jnp.tanh(hx)


def _gelu_tanh(x):
    c = math.sqrt(2.0 / math.pi)
    hx = 0.5 * x
    return hx + hx * jnp.tanh(x * ((x * x) * (0.044715 * c) + c))


def _layer_norm(x, g, b):
    mu = jnp.mean(x, axis=-1, keepdims=True)
    xc = x - mu
    var = jnp.mean(xc * xc, axis=-1, keepdims=True)
    return xc * lax.rsqrt(var + EPS) * g + b


def _rms_norm(x, g):
    ms = jnp.mean(x * x, axis=-1, keepdims=True)
    return (x * lax.rsqrt(ms + EPS)) * g


def _trunk_kernel(x_ref, c_ref, w_ada_ref, b_ada_ref, norm_g_ref, final_g_ref, w_in_ref, w_out_ref,
                  w_conv_a_ref, w_dw_c_ref, pool_scale_ref, b_dw_c_ref, ln_g_c_ref, ln_b_c_ref, b_pw2_c_ref,
                  ln_g_d_ref, ln_b_d_ref, w_pool_ref, w_pw2_ref, w_s_ref, b_s_ref, o_ref,
                  cx_buf, p_buf, s2_buf, s4_buf, s8_buf, h_buf, mod_buf, w_in_buf, w_out_buf, w_s_buf, w_pw2_buf):
    g = pl.program_id(0)
    T = ROW_TILE

    for s in range(N_PREP):
        l, j = divmod(s, PREP_CHUNKS)

        @pl.when(g == s)
        def _(l=l, j=j):
            in_cols = slice(j * PREP_IN_COLS, (j + 1) * PREP_IN_COLS)
            out_cols = slice(j * PREP_OUT_COLS, (j + 1) * PREP_OUT_COLS)
            for k in range(PREP_IN_COLS // GW):
                grp = j * PREP_IN_COLS // GW + k
                stored = Z_GROUP_ORDER.index(grp)
                w_grp = w_in_ref[0, :, k * GW:(k + 1) * GW]
                if grp in HALVED_GROUPS:
                    w_grp = w_grp * 0.5
                w_in_buf[l, :, stored * GW:(stored + 1) * GW] = w_grp.astype(jnp.bfloat16)
            w_out_buf[l, :, out_cols] = w_out_ref[0].astype(jnp.bfloat16)
            if j == 0:
                tri = (lax.broadcasted_iota(jnp.int32, (SGU_CHUNK, SGU_CHUNK), 0)
                       >= lax.broadcasted_iota(jnp.int32, (SGU_CHUNK, SGU_CHUNK), 1))
                w_pw2_buf[l] = w_pw2_ref[l].astype(jnp.bfloat16)
                w_s_buf[l] = jnp.concatenate([jnp.where(tri, w_s_ref[l, hd], 0.0).astype(jnp.bfloat16)
                                              for hd in range(N_HEADS_D)], axis=1)
            ca = _silu(jnp.broadcast_to(c_ref[...], (SUBLANES, D_MODEL))).astype(jnp.bfloat16)
            mod_buf[l, :, in_cols] = (jnp.dot(ca, w_ada_ref[0].astype(jnp.bfloat16),
                                              preferred_element_type=jnp.float32)
                                      + b_ada_ref[l:l + 1, in_cols])

    @pl.when(g == 0)
    def _():
        cx_buf[:, :, 0:HEAD_A, :] = jnp.zeros((DEPTH, N_CT, HEAD_A, LANES), jnp.float32)
        for buf in (p_buf, s2_buf):
            buf[:, :, 0:HEAD_P, :] = jnp.zeros((DEPTH, N_CT, HEAD_P, LANES), jnp.float32)
        for buf in (s4_buf, s8_buf):
            buf[:, 0:HEAD_P, :] = jnp.zeros((DEPTH, HEAD_P, LANES), jnp.float32)
        h_buf[:, :, 0:HEAD_C, :] = jnp.zeros((DEPTH, N_CT, HEAD_C, LANES), jnp.float32)

    @pl.when(g >= N_PREP)
    def _():
        _trunk_step(g - N_PREP, x_ref, mod_buf, norm_g_ref, final_g_ref, w_in_buf, w_out_buf,
                    w_conv_a_ref, w_dw_c_ref, pool_scale_ref, b_dw_c_ref, ln_g_c_ref, ln_b_c_ref, b_pw2_c_ref,
                    ln_g_d_ref, ln_b_d_ref, w_pool_ref, w_pw2_buf, w_s_buf, b_s_ref, o_ref,
                    cx_buf, p_buf, s2_buf, s4_buf, s8_buf, h_buf)


def _trunk_step(i, x_ref, mod_ref, norm_g_ref, final_g_ref, w_in_ref, w_out_ref,
                w_conv_a_ref, w_dw_c_ref, pool_scale_ref, b_dw_c_ref, ln_g_c_ref, ln_b_c_ref, b_pw2_c_ref,
                ln_g_d_ref, ln_b_d_ref, w_pool_ref, w_pw2_ref, w_s_ref, b_s_ref, o_ref,
                cx_buf, p_buf, s2_buf, s4_buf, s8_buf, h_buf):
    T = ROW_TILE
    lane = lax.broadcasted_iota(jnp.int32, (T, LANES), 1)
    left = lane < POOL_GROUP_DIM
    top = max(POOL_WINDOWS)
    row1 = lax.broadcasted_iota(jnp.int32, (top, LANES), 0) + (i * T + 1)
    inv_count = []
    left_top = lax.broadcasted_iota(jnp.int32, (top, LANES), 1) < POOL_GROUP_DIM
    left_row = lax.broadcasted_iota(jnp.int32, (1, LANES), 1) < POOL_GROUP_DIM
    for ct in range(N_CT):
        w_left, w_right = POOL_WINDOWS[2 * ct], POOL_WINDOWS[2 * ct + 1]
        window = jnp.where(left_top, w_left, w_right)
        inv_count.append((1.0 / jnp.minimum(row1, window).astype(jnp.float32),
                          jnp.where(left_row, 1.0 / w_left, 1.0 / w_right)))
    head_of_lane = lax.broadcasted_iota(jnp.int32, (SGU_CHUNK, GW), 1) // SGU_HEAD_DIM

    def vec(l, ref):
        return ref[l:l + 1, :]

    def shifted(buf, idx, head, back):
        return buf[idx + (slice(head - back, head - back + T), slice(None))]

    def carry(buf, idx, head):
        buf[idx + (slice(0, head), slice(None))] = buf[idx + (slice(T, T + head), slice(None))]

    def layer(l, x_cur):
        shift = mod_ref[l, 0:1, 0:D_MODEL]
        scale = mod_ref[l, 0:1, D_MODEL:2 * D_MODEL]
        gate = mod_ref[l, 0:1, 2 * D_MODEL:3 * D_MODEL]
        h = (_rms_norm(x_cur, norm_g_ref[l:l + 1, :] * (1.0 + scale)) + shift).astype(jnp.bfloat16)

        def short_conv(a_b, a_c, a_x, a_gh):
            cx = a_c * a_x
            convs = []
            for ct in range(N_CT):
                lanes = slice(ct * LANES, (ct + 1) * LANES)
                cx_buf[l, ct, HEAD_A:HEAD_A + T, :] = cx[:, lanes]
                conv = None
                for j in range(SHORT_CONV):
                    term = (w_conv_a_ref[l, j:j + 1, lanes]
                            * shifted(cx_buf, (l, ct), HEAD_A, SHORT_CONV - 1 - j))
                    conv = term if conv is None else conv + term
                carry(cx_buf, (l, ct), HEAD_A)
                convs.append(conv)
            return (a_b * jnp.concatenate(convs, axis=-1)) * _silu_of_half(a_gh)

        def pooling(b_p, b_gh):
            pooled = []
            for ct in range(N_CT):
                p = b_p[:, ct * LANES:(ct + 1) * LANES]
                p_buf[l, ct, HEAD_P:HEAD_P + T, :] = p
                s2 = p + shifted(p_buf, (l, ct), HEAD_P, 1)
                s2_buf[l, ct, HEAD_P:HEAD_P + T, :] = s2
                s4 = s2 + shifted(s2_buf, (l, ct), HEAD_P, 2)
                carry(p_buf, (l, ct), HEAD_P)
                carry(s2_buf, (l, ct), HEAD_P)
                if ct == 0:
                    wsum = jnp.where(left, s2, s4)
                else:
                    s4_buf[l, HEAD_P:HEAD_P + T, :] = s4
                    s8 = s4 + shifted(s4_buf, (l,), HEAD_P, 4)
                    s8_buf[l, HEAD_P:HEAD_P + T, :] = s8
                    s16 = s8 + shifted(s8_buf, (l,), HEAD_P, 8)
                    carry(s4_buf, (l,), HEAD_P)
                    carry(s8_buf, (l,), HEAD_P)
                    wsum = jnp.where(left, s8, s16)
                inv_top, inv_rest = inv_count[ct]
                pooled.append(jnp.concatenate([wsum[0:top] * inv_top - p[0:top],
                                               wsum[top:] * inv_rest - p[top:]], axis=0))
            pooled = jnp.concatenate(pooled, axis=-1)
            y_b = jnp.dot(pooled.astype(jnp.bfloat16), w_pool_ref[l], preferred_element_type=jnp.float32)
            return (y_b * vec(l, pool_scale_ref)) * _silu_of_half(b_gh)

        def conformer(c_a, c_glh, c_gh):
            hgl = c_a * _sigmoid_of_half(c_glh)
            accs = []
            for ct in range(N_CT):
                lanes = slice(ct * LANES, (ct + 1) * LANES)
                h_buf[l, ct, HEAD_C:HEAD_C + T, :] = hgl[:, lanes]
                acc = None
                for j in range(CONF_WIDTH):
                    term = (w_dw_c_ref[l, j:j + 1, lanes]
                            * shifted(h_buf, (l, ct), HEAD_C, CONF_WIDTH - 1 - j))
                    acc = term if acc is None else acc + term
                carry(h_buf, (l, ct), HEAD_C)
                accs.append(acc)
            hc = jnp.concatenate(accs, axis=-1) + vec(l, b_dw_c_ref)
            hc = _silu(_layer_norm(hc, vec(l, ln_g_c_ref), vec(l, ln_b_c_ref)))
            y_c = jnp.dot(hc.astype(jnp.bfloat16), w_pw2_ref[l], preferred_element_type=jnp.float32)
            return (y_c + vec(l, b_pw2_c_ref)) * _silu_of_half(c_gh)

        def sgu(d_u, d_v, d_gh):
            u = _gelu_tanh(d_u)
            v = _layer_norm(_gelu_tanh(d_v), vec(l, ln_g_d_ref), vec(l, ln_b_d_ref))
            mixed_chunks = []
            for ck in range(T // SGU_CHUNK):
                v_ck = v[ck * SGU_CHUNK:(ck + 1) * SGU_CHUNK, :]
                stacked = jnp.concatenate([jnp.where(head_of_lane == hd, v_ck, 0.0).astype(jnp.bfloat16)
                                           for hd in range(N_HEADS_D)], axis=0)
                mixed_chunks.append(jnp.dot(w_s_ref[l], stacked, preferred_element_type=jnp.float32)
                                    + b_s_ref[l])
            mixed = jnp.concatenate(mixed_chunks, axis=0)
            return (u * mixed) * _silu_of_half(d_gh)

        z = jnp.dot(h, w_in_ref[l], preferred_element_type=jnp.float32)
        zs = [z[:, s * GW:(s + 1) * GW] for s in (Z_GROUP_ORDER.index(k) for k in range(D_IN // GW))]
        y_c = conformer(*zs[6:9])
        y_a = short_conv(*zs[0:4])
        y_d = sgu(*zs[9:12])
        y_b = pooling(*zs[4:6])
        y_cat = jnp.concatenate([y_a, y_b, y_c, y_d], axis=-1).astype(jnp.bfloat16)
        y = jnp.dot(y_cat, w_out_ref[l], preferred_element_type=jnp.float32)
        return x_cur + gate * y

    x_cur = x_ref[...]
    for l in range(DEPTH):
        x_cur = layer(l, x_cur)
    o_ref[...] = _rms_norm(x_cur, final_g_ref[...])


def _const_spec(shape):
    zeros = (0,) * len(shape)
    return pl.BlockSpec(shape, lambda i: zeros, pipeline_mode=pl.Buffered(1))


def kernel(x, c, norm_g, w_ada, b_ada, w_in, w_conv_a, w_pool, pool_scale, w_dw_c, b_dw_c, ln_g_c, ln_b_c, w_pw2_c, b_pw2_c, ln_g_d, ln_b_d, w_s_d, b_s_d, w_out, final_g):
    f32 = jnp.float32
    assert x.shape == (1, SEQ, D_MODEL) and c.shape == (1, D_MODEL)
    assert POOL_WINDOWS == (2, 4, 8, 16) and N_CT == 2

    n_grp = len(POOL_WINDOWS)
    w_pool_bd = (w_pool[:, :, :, None, :] * jnp.eye(n_grp, dtype=f32)[None, :, None, :, None]
                 ).reshape(DEPTH, GW, GW).astype(jnp.bfloat16)
    b_s_full = jnp.broadcast_to(jnp.swapaxes(b_s_d, 1, 2)[:, :, :, None],
                                (DEPTH, SGU_CHUNK, N_HEADS_D, SGU_HEAD_DIM)).reshape(DEPTH, SGU_CHUNK, GW)
    vec_specs = [_const_spec((DEPTH, GW))] * 7

    T = ROW_TILE

    def row_tile(g):
        return (jnp.maximum(g - N_PREP, 0), 0)

    def weight_chunk(g):
        s = jnp.minimum(g, N_PREP - 1)
        return (s // PREP_CHUNKS, 0, s % PREP_CHUNKS)

    out = pl.pallas_call(
        _trunk_kernel,
        grid=(N_PREP + SEQ // T,),
        in_specs=[
            pl.BlockSpec((T, D_MODEL), row_tile),
            _const_spec((1, D_MODEL)),
            pl.BlockSpec((1, D_MODEL, PREP_IN_COLS), weight_chunk),
            _const_spec((DEPTH, 3 * D_MODEL)),
            _const_spec((DEPTH, D_MODEL)),
            _const_spec((1, D_MODEL)),
            pl.BlockSpec((1, D_MODEL, PREP_IN_COLS), weight_chunk),
            pl.BlockSpec((1, D_MODEL, PREP_OUT_COLS), weight_chunk),
            _const_spec((DEPTH, SHORT_CONV, GW)),
            _const_spec((DEPTH, CONF_WIDTH, GW)),
            *vec_specs,
            _const_spec((DEPTH, GW, GW)),
            _const_spec((DEPTH, GW, GW)),
            _const_spec((DEPTH, N_HEADS_D, SGU_CHUNK, SGU_CHUNK)),
            _const_spec((DEPTH, SGU_CHUNK, GW)),
        ],
        out_specs=pl.BlockSpec((T, D_MODEL), row_tile),
        out_shape=jax.ShapeDtypeStruct((SEQ, D_MODEL), f32),
        scratch_shapes=[
            pltpu.VMEM((DEPTH, N_CT, HEAD_A + T, LANES), f32),
            pltpu.VMEM((DEPTH, N_CT, HEAD_P + T, LANES), f32),
            pltpu.VMEM((DEPTH, N_CT, HEAD_P + T, LANES), f32),
            pltpu.VMEM((DEPTH, HEAD_P + T, LANES), f32),
            pltpu.VMEM((DEPTH, HEAD_P + T, LANES), f32),
            pltpu.VMEM((DEPTH, N_CT, HEAD_C + T, LANES), f32),
            pltpu.VMEM((DEPTH, SUBLANES, 3 * D_MODEL), f32),
            pltpu.VMEM((DEPTH, D_MODEL, D_IN), jnp.bfloat16),
            pltpu.VMEM((DEPTH, D_MODEL, D_MODEL), jnp.bfloat16),
            pltpu.VMEM((DEPTH, SGU_CHUNK, N_HEADS_D * SGU_CHUNK), jnp.bfloat16),
            pltpu.VMEM((DEPTH, GW, GW), jnp.bfloat16),
        ],
        compiler_params=pltpu.CompilerParams(
            dimension_semantics=("arbitrary",),
            vmem_limit_bytes=VMEM_LIMIT_BYTES),
        name="hybrid_trunk",
    )(x.reshape(SEQ, D_MODEL), c, w_ada, b_ada, norm_g, final_g.reshape(1, D_MODEL),
      w_in, w_out, w_conv_a, w_dw_c,
      pool_scale, b_dw_c, ln_g_c, ln_b_c, b_pw2_c, ln_g_d, ln_b_d,
      w_pool_bd, w_pw2_c, w_s_d, b_s_full)
    return out.reshape(1, SEQ, D_MODEL)
```

```python
import math

import jax
import jax.numpy as jnp
from jax import lax
from jax.experimental import pallas as pl
from jax.experimental.pallas import tpu as pltpu

D_MODEL = 1024
SEQ = 16384
DEPTH = 2
GW = 256
D_IN = 12 * GW
SHORT_CONV = 3
CONF_WIDTH = 31
POOL_WINDOWS = (2, 4, 8, 16)
POOL_GROUP_DIM = GW // len(POOL_WINDOWS)
SGU_CHUNK = 128
N_HEADS_D = 4
SGU_HEAD_DIM = GW // N_HEADS_D
EPS = 1e-6

SUBLANES = 8
LANES = 128
N_CT = GW // LANES
ROW_TILE = 512
HEAD_A = 8
HEAD_P = 8
HEAD_C = 32
VMEM_LIMIT_BYTES = 56 * 1024 * 1024
HALVED_GROUPS = (3, 5, 7, 8, 11)
PREP_CHUNKS = 4
N_PREP = DEPTH * PREP_CHUNKS
PREP_IN_COLS = D_IN // PREP_CHUNKS
PREP_OUT_COLS = D_MODEL // PREP_CHUNKS
Z_GROUP_ORDER = (6, 7, 1, 2, 0, 3, 10, 9, 11, 4, 5, 8)


def _silu(x):
    hx = 0.5 * x
    return hx + hx * jnp.tanh(hx)


def _silu_of_half(hx):
    return hx + hx * jnp.tanh(hx)


def _sigmoid_of_half(hx):
    return 0.5 + 0.5 * jnp.tanh(hx)


def _gelu_tanh(x):
    c = math.sqrt(2.0 / math.pi)
    hx = 0.5 * x
    return hx + hx * jnp.tanh(x * ((x * x) * (0.044715 * c) + c))


def _layer_norm(x, g, b):
    mu = jnp.mean(x, axis=-1, keepdims=True)
    xc = x - mu
    var = jnp.mean(xc * xc, axis=-1, keepdims=True)
    return xc * lax.rsqrt(var + EPS) * g + b


def _rms_norm(x, g):
    ms = jnp.mean(x * x, axis=-1, keepdims=True)
    return (x * lax.rsqrt(ms + EPS)) * g


def _trunk_kernel(x_ref, c_ref, w_ada_ref, b_ada_ref, norm_g_ref, final_g_ref, w_in_ref, w_out_ref,
                  w_conv_a_ref, w_dw_c_ref, pool_scale_ref, b_dw_c_ref, ln_g_c_ref, ln_b_c_ref, b_pw2_c_ref,
                  ln_g_d_ref, ln_b_d_ref, w_pool_ref, w_pw2_ref, w_s_ref, b_s_ref, o_ref,
                  cx_buf, p_buf, s2_buf, s4_buf, s8_buf, h_buf, mod_buf, w_in_buf, w_out_buf, w_s_buf, w_pw2_buf):
    g = pl.program_id(0)
    T = ROW_TILE

    for s in range(N_PREP):
        l, j = divmod(s, PREP_CHUNKS)

        @pl.when(g == s)
        def _(l=l, j=j):
            in_cols = slice(j * PREP_IN_COLS, (j + 1) * PREP_IN_COLS)
            out_cols = slice(j * PREP_OUT_COLS, (j + 1) * PREP_OUT_COLS)
            for k in range(PREP_IN_COLS // GW):
                grp = j * PREP_IN_COLS // GW + k
                stored = Z_GROUP_ORDER.index(grp)
                w_grp = w_in_ref[0, :, k * GW:(k + 1) * GW]
                if grp in HALVED_GROUPS:
                    w_grp = w_grp * 0.5
                w_in_buf[l, :, stored * GW:(stored + 1) * GW] = w_grp.astype(jnp.bfloat16)
            w_out_buf[l, :, out_cols] = w_out_ref[0].astype(jnp.bfloat16)
            if j == 0:
                tri = (lax.broadcasted_iota(jnp.int32, (SGU_CHUNK, SGU_CHUNK), 0)
                       >= lax.broadcasted_iota(jnp.int32, (SGU_CHUNK, SGU_CHUNK), 1))
                w_pw2_buf[l] = w_pw2_ref[l].astype(jnp.bfloat16)
                w_s_buf[l] = jnp.concatenate([jnp.where(tri, w_s_ref[l, hd], 0.0).astype(jnp.bfloat16)
                                              for hd in range(N_HEADS_D)], axis=1)
            ca = _silu(jnp.broadcast_to(c_ref[...], (SUBLANES, D_MODEL))).astype(jnp.bfloat16)
            mod_buf[l, :, in_cols] = (jnp.dot(ca, w_ada_ref[0].astype(jnp.bfloat16),
                                              preferred_element_type=jnp.float32)
                                      + b_ada_ref[l:l + 1, in_cols])

    @pl.when(g == 0)
    def _():
        cx_buf[:, :, 0:HEAD_A, :] = jnp.zeros((DEPTH, N_CT, HEAD_A, LANES), jnp.float32)
        for buf in (p_buf, s2_buf):
            buf[:, :, 0:HEAD_P, :] = jnp.zeros((DEPTH, N_CT, HEAD_P, LANES), jnp.float32)
        for buf in (s4_buf, s8_buf):
            buf[:, 0:HEAD_P, :] = jnp.zeros((DEPTH, HEAD_P, LANES), jnp.float32)
        h_buf[:, :, 0:HEAD_C, :] = jnp.zeros((DEPTH, N_CT, HEAD_C, LANES), jnp.float32)

    @pl.when(g >= N_PREP)
    def _():
        _trunk_step(g - N_PREP, x_ref, mod_buf, norm_g_ref, final_g_ref, w_in_buf, w_out_buf,
                    w_conv_a_ref, w_dw_c_ref, pool_scale_ref, b_dw_c_ref, ln_g_c_ref, ln_b_c_ref, b_pw2_c_ref,
                    ln_g_d_ref, ln_b_d_ref, w_pool_ref, w_pw2_buf, w_s_buf, b_s_ref, o_ref,
                    cx_buf, p_buf, s2_buf, s4_buf, s8_buf, h_buf)


def _trunk_step(i, x_ref, mod_ref, norm_g_ref, final_g_ref, w_in_ref, w_out_ref,
                w_conv_a_ref, w_dw_c_ref, pool_scale_ref, b_dw_c_ref, ln_g_c_ref, ln_b_c_ref, b_pw2_c_ref,
                ln_g_d_ref, ln_b_d_ref, w_pool_ref, w_pw2_ref, w_s_ref, b_s_ref, o_ref,
                cx_buf, p_buf, s2_buf, s4_buf, s8_buf, h_buf):
    T = ROW_TILE
    lane = lax.broadcasted_iota(jnp.int32, (T, LANES), 1)
    left = lane < POOL_GROUP_DIM
    top = max(POOL_WINDOWS)
    row1 = lax.broadcasted_iota(jnp.int32, (top, LANES), 0) + (i * T + 1)
    inv_count = []
    left_top = lax.broadcasted_iota(jnp.int32, (top, LANES), 1) < POOL_GROUP_DIM
    left_row = lax.broadcasted_iota(jnp.int32, (1, LANES), 1) < POOL_GROUP_DIM
    for ct in range(N_CT):
        w_left, w_right = POOL_WINDOWS[2 * ct], POOL_WINDOWS[2 * ct + 1]
        window = jnp.where(left_top, w_left, w_right)
        inv_count.append((1.0 / jnp.minimum(row1, window).astype(jnp.float32),
                          jnp.where(left_row, 1.0 / w_left, 1.0 / w_right)))
    head_of_lane = lax.broadcasted_iota(jnp.int32, (SGU_CHUNK, GW), 1) // SGU_HEAD_DIM

    def vec(l, ref):
        return ref[l:l + 1, :]

    def shifted(buf, idx, head, back):
        return buf[idx + (slice(head - back, head - back + T), slice(None))]

    def carry(buf, idx, head):
        buf[idx + (slice(0, head), slice(None))] = buf[idx + (slice(T, T + head), slice(None))]

    def layer(l, x_cur):
        shift = mod_ref[l, 0:1, 0:D_MODEL]
        scale = mod_ref[l, 0:1, D_MODEL:2 * D_MODEL]
        gate = mod_ref[l, 0:1, 2 * D_MODEL:3 * D_MODEL]
        h = (_rms_norm(x_cur, norm_g_ref[l:l + 1, :] * (1.0 + scale)) + shift).astype(jnp.bfloat16)

        def short_conv(a_b, a_c, a_x, a_gh):
            cx = a_c * a_x
            convs = []
            for ct in range(N_CT):
                lanes = slice(ct * LANES, (ct + 1) * LANES)
                cx_buf[l, ct, HEAD_A:HEAD_A + T, :] = cx[:, lanes]
                conv = None
                for j in range(SHORT_CONV):
                    term = (w_conv_a_ref[l, j:j + 1, lanes]
                            * shifted(cx_buf, (l, ct), HEAD_A, SHORT_CONV - 1 - j))
                    conv = term if conv is None else conv + term
                carry(cx_buf, (l, ct), HEAD_A)
                convs.append(conv)
            return (a_b * jnp.concatenate(convs, axis=-1)) * _silu_of_half(a_gh)

        def pooling(b_p, b_gh):
            pooled = []
            for ct in range(N_CT):
                p = b_p[:, ct * LANES:(ct + 1) * LANES]
                p_buf[l, ct, HEAD_P:HEAD_P + T, :] = p
                s2 = p + shifted(p_buf, (l, ct), HEAD_P, 1)
                s2_buf[l, ct, HEAD_P:HEAD_P + T, :] = s2
                s4 = s2 + shifted(s2_buf, (l, ct), HEAD_P, 2)
                carry(p_buf, (l, ct), HEAD_P)
                carry(s2_buf, (l, ct), HEAD_P)
                if ct == 0:
                    wsum = jnp.where(left, s2, s4)
                else:
                    s4_buf[l, HEAD_P:HEAD_P + T, :] = s4
                    s8 = s4 + shifted(s4_buf, (l,), HEAD_P, 4)
                    s8_buf[l, HEAD_P:HEAD_P + T, :] = s8
                    s16 = s8 + shifted(s8_buf, (l,), HEAD_P, 8)
                    carry(s4_buf, (l,), HEAD_P)
                    carry(s8_buf, (l,), HEAD_P)
                    wsum = jnp.where(left, s8, s16)
                inv_top, inv_rest = inv_count[ct]
                pooled.append(jnp.concatenate([wsum[0:top] * inv_top - p[0:top],
                                               wsum[top:] * inv_rest - p[top:]], axis=0))
            pooled = jnp.concatenate(pooled, axis=-1)
            y_b = jnp.dot(pooled.astype(jnp.bfloat16), w_pool_ref[l], preferred_element_type=jnp.float32)
            return (y_b * vec(l, pool_scale_ref)) * _silu_of_half(b_gh)

        def conformer(c_a, c_glh, c_gh):
            hgl = c_a * _sigmoid_of_half(c_glh)
            accs = []
            for ct in range(N_CT):
                lanes = slice(ct * LANES, (ct + 1) * LANES)
                h_buf[l, ct, HEAD_C:HEAD_C + T, :] = hgl[:, lanes]
                acc = None
                for j in range(CONF_WIDTH):
                    term = (w_dw_c_ref[l, j:j + 1, lanes]
                            * shifted(h_buf, (l, ct), HEAD_C, CONF_WIDTH - 1 - j))
                    acc = term if acc is None else acc + term
                carry(h_buf, (l, ct), HEAD_C)
                accs.append(acc)
            hc = jnp.concatenate(accs, axis=-1) + vec(l, b_dw_c_ref)
            hc = _silu(_layer_norm(hc, vec(l, ln_g_c_ref), vec(l, ln_b_c_ref)))
            y_c = jnp.dot(hc.astype(jnp.bfloat16), w_pw2_ref[l], preferred_element_type=jnp.float32)
            return (y_c + vec(l, b_pw2_c_ref)) * _silu_of_half(c_gh)

        def sgu(d_u, d_v, d_gh):
            u = _gelu_tanh(d_u)
            v = _layer_norm(_gelu_tanh(d_v), vec(l, ln_g_d_ref), vec(l, ln_b_d_ref))
            mixed_chunks = []
            for ck in range(T // SGU_CHUNK):
                v_ck = v[ck * SGU_CHUNK:(ck + 1) * SGU_CHUNK, :]
                stacked = jnp.concatenate([jnp.where(head_of_lane == hd, v_ck, 0.0).astype(jnp.bfloat16)
                                           for hd in range(N_HEADS_D)], axis=0)
                mixed_chunks.append(jnp.dot(w_s_ref[l], stacked, preferred_element_type=jnp.float32)
                                    + b_s_ref[l])
            mixed = jnp.concatenate(mixed_chunks, axis=0)
            return (u * mixed) * _silu_of_half(d_gh)

        z = jnp.concatenate([jnp.dot(h[r * (T // 2):(r + 1) * (T // 2)], w_in_ref[l],
                                     preferred_element_type=jnp.float32) for r in range(2)], axis=0)
        zs = [z[:, s * GW:(s + 1) * GW] for s in (Z_GROUP_ORDER.index(k) for k in range(D_IN // GW))]
        y_c = conformer(*zs[6:9])
        y_a = short_conv(*zs[0:4])
        y_d = sgu(*zs[9:12])
        y_b = pooling(*zs[4:6])
        y_cat = jnp.concatenate([y_a, y_b, y_c, y_d], axis=-1).astype(jnp.bfloat16)
        y = jnp.concatenate([jnp.dot(y_cat[r * (T // 2):(r + 1) * (T // 2)], w_out_ref[l],
                                     preferred_element_type=jnp.float32) for r in range(2)], axis=0)
        return x_cur + gate * y

    x_cur = x_ref[...]
    for l in range(DEPTH):
        x_cur = layer(l, x_cur)
    o_ref[...] = _rms_norm(x_cur, final_g_ref[...])


def _const_spec(shape):
    zeros = (0,) * len(shape)
    return pl.BlockSpec(shape, lambda i: zeros, pipeline_mode=pl.Buffered(1))


def kernel(x, c, norm_g, w_ada, b_ada, w_in, w_conv_a, w_pool, pool_scale, w_dw_c, b_dw_c, ln_g_c, ln_b_c, w_pw2_c, b_pw2_c, ln_g_d, ln_b_d, w_s_d, b_s_d, w_out, final_g):
    f32 = jnp.float32
    assert x.shape == (1, SEQ, D_MODEL) and c.shape == (1, D_MODEL)
    assert POOL_WINDOWS == (2, 4, 8, 16) and N_CT == 2

    n_grp = len(POOL_WINDOWS)
    w_pool_bd = (w_pool[:, :, :, None, :] * jnp.eye(n_grp, dtype=f32)[None, :, None, :, None]
                 ).reshape(DEPTH, GW, GW).astype(jnp.bfloat16)
    b_s_full = jnp.broadcast_to(jnp.swapaxes(b_s_d, 1, 2)[:, :, :, None],
                                (DEPTH, SGU_CHUNK, N_HEADS_D, SGU_HEAD_DIM)).reshape(DEPTH, SGU_CHUNK, GW)
    vec_specs = [_const_spec((DEPTH, GW))] * 7

    T = ROW_TILE

    def row_tile(g):
        return (jnp.maximum(g - N_PREP, 0), 0)

    def weight_chunk(g):
        s = jnp.minimum(g, N_PREP - 1)
        return (s // PREP_CHUNKS, 0, s % PREP_CHUNKS)

    out = pl.pallas_call(
        _trunk_kernel,
        grid=(N_PREP + SEQ // T,),
        in_specs=[
            pl.BlockSpec((T, D_MODEL), row_tile),
            _const_spec((1, D_MODEL)),
            pl.BlockSpec((1, D_MODEL, PREP_IN_COLS), weight_chunk),
            _const_spec((DEPTH, 3 * D_MODEL)),
            _const_spec((DEPTH, D_MODEL)),
            _const_spec((1, D_MODEL)),
            pl.BlockSpec((1, D_MODEL, PREP_IN_COLS), weight_chunk),
            pl.BlockSpec((1, D_MODEL, PREP_OUT_COLS), weight_chunk),
            _const_spec((DEPTH, SHORT_CONV, GW)),
            _const_spec((DEPTH, CONF_WIDTH, GW)),
            *vec_specs,
            _const_spec((DEPTH, GW, GW)),
            _const_spec((DEPTH, GW, GW)),
            _const_spec((DEPTH, N_HEADS_D, SGU_CHUNK, SGU_CHUNK)),
            _const_spec((DEPTH, SGU_CHUNK, GW)),
        ],
        out_specs=pl.BlockSpec((T, D_MODEL), row_tile),
        out_shape=jax.ShapeDtypeStruct((SEQ, D_MODEL), f32),
        scratch_shapes=[
            pltpu.VMEM((DEPTH, N_CT, HEAD_A + T, LANES), f32),
            pltpu.VMEM((DEPTH, N_CT, HEAD_P + T, LANES), f32),
            pltpu.VMEM((DEPTH, N_CT, HEAD_P + T, LANES), f32),
            pltpu.VMEM((DEPTH, HEAD_P + T, LANES), f32),
            pltpu.VMEM((DEPTH, HEAD_P + T, LANES), f32),
            pltpu.VMEM((DEPTH, N_CT, HEAD_C + T, LANES), f32),
            pltpu.VMEM((DEPTH, SUBLANES, 3 * D_MODEL), f32),
            pltpu.VMEM((DEPTH, D_MODEL, D_IN), jnp.bfloat16),
            pltpu.VMEM((DEPTH, D_MODEL, D_MODEL), jnp.bfloat16),
            pltpu.VMEM((DEPTH, SGU_CHUNK, N_HEADS_D * SGU_CHUNK), jnp.bfloat16),
            pltpu.VMEM((DEPTH, GW, GW), jnp.bfloat16),
        ],
        compiler_params=pltpu.CompilerParams(
            dimension_semantics=("arbitrary",),
            vmem_limit_bytes=VMEM_LIMIT_BYTES),
        name="hybrid_trunk",
    )(x.reshape(SEQ, D_MODEL), c, w_ada, b_ada, norm_g, final_g.reshape(1, D_MODEL),
      w_in, w_out, w_conv_a, w_dw_c,
      pool_scale, b_dw_c, ln_g_c, ln_b_c, b_pw2_c, ln_g_d, ln_b_d,
      w_pool_bd, w_pw2_c, w_s_d, b_s_full)
    return out.reshape(1, SEQ, D_MODEL)
```

```python
import math

import jax
import jax.numpy as jnp
from jax import lax
from jax.experimental import pallas as pl
from jax.experimental.pallas import tpu as pltpu

D_MODEL = 1024
SEQ = 16384
DEPTH = 2
GW = 256
D_IN = 12 * GW
SHORT_CONV = 3
CONF_WIDTH = 31
POOL_WINDOWS = (2, 4, 8, 16)
POOL_GROUP_DIM = GW // len(POOL_WINDOWS)
SGU_CHUNK = 128
N_HEADS_D = 4
SGU_HEAD_DIM = GW // N_HEADS_D
EPS = 1e-6

SUBLANES = 8
LANES = 128
N_CT = GW // LANES
ROW_TILE = 512
HEAD_A = 8
HEAD_P = 8
HEAD_C = 32
VMEM_LIMIT_BYTES = 56 * 1024 * 1024
HALVED_GROUPS = (3, 5, 7, 8, 11)
PREP_CHUNKS = 4
N_PREP = DEPTH * PREP_CHUNKS
PREP_IN_COLS = D_IN // PREP_CHUNKS
PREP_OUT_COLS = D_MODEL // PREP_CHUNKS
Z_GROUP_ORDER = (6, 7, 1, 2, 0, 3, 10, 9, 11, 4, 5, 8)


def _silu(x):
    hx = 0.5 * x
    return hx + hx * jnp.tanh(hx)


def _silu_of_half(hx):
    return hx + hx * jnp.tanh(hx)


def _sigmoid_of_half(hx):
    return 0.5 + 0.5 * jnp.tanh(hx)


def _gelu_tanh(x):
    c = math.sqrt(2.0 / math.pi)
    hx = 0.5 * x
    return hx + hx * jnp.tanh(x * ((x * x) * (0.044715 * c) + c))


def _layer_norm(x, g, b):
    mu = jnp.mean(x, axis=-1, keepdims=True)
    xc = x - mu
    var = jnp.mean(xc * xc, axis=-1, keepdims=True)
    return xc * lax.rsqrt(var + EPS) * g + b


def _rms_norm(x, g):
    ms = jnp.mean(x * x, axis=-1, keepdims=True)
    return (x * lax.rsqrt(ms + EPS)) * g


def _trunk_kernel(x_ref, c_ref, w_ada_ref, b_ada_ref, norm_g_ref, final_g_ref, w_in_ref, w_out_ref,
                  w_conv_a_ref, w_dw_c_ref, pool_scale_ref, b_dw_c_ref, ln_g_c_ref, ln_b_c_ref, b_pw2_c_ref,
                  ln_g_d_ref, ln_b_d_ref, w_pool_ref, w_pw2_ref, w_s_ref, b_s_ref, o_ref,
                  cx_buf, p_buf, s2_buf, s4_buf, s8_buf, h_buf, mod_buf, w_in_buf, w_out_buf, w_s_buf, w_pw2_buf):
    g = pl.program_id(0)
    T = ROW_TILE

    for s in range(N_PREP):
        l, j = divmod(s, PREP_CHUNKS)

        @pl.when(g == s)
        def _(l=l, j=j):
            in_cols = slice(j * PREP_IN_COLS, (j + 1) * PREP_IN_COLS)
            out_cols = slice(j * PREP_OUT_COLS, (j + 1) * PREP_OUT_COLS)
            for k in range(PREP_IN_COLS // GW):
                grp = j * PREP_IN_COLS // GW + k
                stored = Z_GROUP_ORDER.index(grp)
                w_grp = w_in_ref[0, :, k * GW:(k + 1) * GW]
                if grp in HALVED_GROUPS:
                    w_grp = w_grp * 0.5
                w_in_buf[l, :, stored * GW:(stored + 1) * GW] = w_grp.astype(jnp.bfloat16)
            w_out_buf[l, :, out_cols] = w_out_ref[0].astype(jnp.bfloat16)
            if j == 0:
                tri = (lax.broadcasted_iota(jnp.int32, (SGU_CHUNK, SGU_CHUNK), 0)
                       >= lax.broadcasted_iota(jnp.int32, (SGU_CHUNK, SGU_CHUNK), 1))
                w_pw2_buf[l] = w_pw2_ref[l].astype(jnp.bfloat16)
                w_s_buf[l] = jnp.concatenate([jnp.where(tri, w_s_ref[l, hd], 0.0).astype(jnp.bfloat16)
                                              for hd in range(N_HEADS_D)], axis=1)
            ca = _silu(jnp.broadcast_to(c_ref[...], (SUBLANES, D_MODEL))).astype(jnp.bfloat16)
            mod_buf[l, :, in_cols] = (jnp.dot(ca, w_ada_ref[0].astype(jnp.bfloat16),
                                              preferred_element_type=jnp.float32)
                                      + b_ada_ref[l:l + 1, in_cols])

    @pl.when(g == 0)
    def _():
        cx_buf[:, :, 0:HEAD_A, :] = jnp.zeros((DEPTH, N_CT, HEAD_A, LANES), jnp.float32)
        for buf in (p_buf, s2_buf):
            buf[:, :, 0:HEAD_P, :] = jnp.zeros((DEPTH, N_CT, HEAD_P, LANES), jnp.float32)
        for buf in (s4_buf, s8_buf):
            buf[:, 0:HEAD_P, :] = jnp.zeros((DEPTH, HEAD_P, LANES), jnp.float32)
        h_buf[:, :, 0:HEAD_C, :] = jnp.zeros((DEPTH, N_CT, HEAD_C, LANES), jnp.float32)

    @pl.when(g >= N_PREP)
    def _():
        _trunk_step(g - N_PREP, x_ref, mod_buf, norm_g_ref, final_g_ref, w_in_buf, w_out_buf,
                    w_conv_a_ref, w_dw_c_ref, pool_scale_ref, b_dw_c_ref, ln_g_c_ref, ln_b_c_ref, b_pw2_c_ref,
                    ln_g_d_ref, ln_b_d_ref, w_pool_ref, w_pw2_buf, w_s_buf, b_s_ref, o_ref,
                    cx_buf, p_buf, s2_buf, s4_buf, s8_buf, h_buf)


def _trunk_step(i, x_ref, mod_ref, norm_g_ref, final_g_ref, w_in_ref, w_out_ref,
                w_conv_a_ref, w_dw_c_ref, pool_scale_ref, b_dw_c_ref, ln_g_c_ref, ln_b_c_ref, b_pw2_c_ref,
                ln_g_d_ref, ln_b_d_ref, w_pool_ref, w_pw2_ref, w_s_ref, b_s_ref, o_ref,
                cx_buf, p_buf, s2_buf, s4_buf, s8_buf, h_buf):
    T = ROW_TILE
    lane = lax.broadcasted_iota(jnp.int32, (T, LANES), 1)
    left = lane < POOL_GROUP_DIM
    top = max(POOL_WINDOWS)
    row1 = lax.broadcasted_iota(jnp.int32, (top, LANES), 0) + (i * T + 1)
    inv_count = []
    left_top = lax.broadcasted_iota(jnp.int32, (top, LANES), 1) < POOL_GROUP_DIM
    left_row = lax.broadcasted_iota(jnp.int32, (1, LANES), 1) < POOL_GROUP_DIM
    for ct in range(N_CT):
        w_left, w_right = POOL_WINDOWS[2 * ct], POOL_WINDOWS[2 * ct + 1]
        window = jnp.where(left_top, w_left, w_right)
        inv_count.append((1.0 / jnp.minimum(row1, window).astype(jnp.float32),
                          jnp.where(left_row, 1.0 / w_left, 1.0 / w_right)))
    head_of_lane = lax.broadcasted_iota(jnp.int32, (SGU_CHUNK, GW), 1) // SGU_HEAD_DIM

    def vec(l, ref):
        return ref[l:l + 1, :]

    def shifted(buf, idx, head, back):
        return buf[idx + (slice(head - back, head - back + T), slice(None))]

    def carry(buf, idx, head):
        buf[idx + (slice(0, head), slice(None))] = buf[idx + (slice(T, T + head), slice(None))]

    def dot_by_halves(lhs, rhs):
        return jnp.concatenate([jnp.dot(lhs[r * (T // 2):(r + 1) * (T // 2)], rhs,
                                        preferred_element_type=jnp.float32) for r in range(2)], axis=0)

    def layer(l, x_cur):
        shift = mod_ref[l, 0:1, 0:D_MODEL]
        scale = mod_ref[l, 0:1, D_MODEL:2 * D_MODEL]
        gate = mod_ref[l, 0:1, 2 * D_MODEL:3 * D_MODEL]
        h = (_rms_norm(x_cur, norm_g_ref[l:l + 1, :] * (1.0 + scale)) + shift).astype(jnp.bfloat16)

        def short_conv(a_b, a_c, a_x, a_gh):
            cx = a_c * a_x
            convs = []
            for ct in range(N_CT):
                lanes = slice(ct * LANES, (ct + 1) * LANES)
                cx_buf[l, ct, HEAD_A:HEAD_A + T, :] = cx[:, lanes]
                conv = None
                for j in range(SHORT_CONV):
                    term = (w_conv_a_ref[l, j:j + 1, lanes]
                            * shifted(cx_buf, (l, ct), HEAD_A, SHORT_CONV - 1 - j))
                    conv = term if conv is None else conv + term
                carry(cx_buf, (l, ct), HEAD_A)
                convs.append(conv)
            return (a_b * jnp.concatenate(convs, axis=-1)) * _silu_of_half(a_gh)

        def pooling(b_p, b_gh):
            pooled = []
            for ct in range(N_CT):
                p = b_p[:, ct * LANES:(ct + 1) * LANES]
                p_buf[l, ct, HEAD_P:HEAD_P + T, :] = p
                s2 = p + shifted(p_buf, (l, ct), HEAD_P, 1)
                s2_buf[l, ct, HEAD_P:HEAD_P + T, :] = s2
                s4 = s2 + shifted(s2_buf, (l, ct), HEAD_P, 2)
                carry(p_buf, (l, ct), HEAD_P)
                carry(s2_buf, (l, ct), HEAD_P)
                if ct == 0:
                    wsum = jnp.where(left, s2, s4)
                else:
                    s4_buf[l, HEAD_P:HEAD_P + T, :] = s4
                    s8 = s4 + shifted(s4_buf, (l,), HEAD_P, 4)
                    s8_buf[l, HEAD_P:HEAD_P + T, :] = s8
                    s16 = s8 + shifted(s8_buf, (l,), HEAD_P, 8)
                    carry(s4_buf, (l,), HEAD_P)
                    carry(s8_buf, (l,), HEAD_P)
                    wsum = jnp.where(left, s8, s16)
                inv_top, inv_rest = inv_count[ct]
                pooled.append(jnp.concatenate([wsum[0:top] * inv_top - p[0:top],
                                               wsum[top:] * inv_rest - p[top:]], axis=0))
            pooled = jnp.concatenate(pooled, axis=-1)
            y_b = dot_by_halves(pooled.astype(jnp.bfloat16), w_pool_ref[l])
            return (y_b * vec(l, pool_scale_ref)) * _silu_of_half(b_gh)

        def conformer(c_a, c_glh, c_gh):
            hgl = c_a * _sigmoid_of_half(c_glh)
            accs = []
            for ct in range(N_CT):
                lanes = slice(ct * LANES, (ct + 1) * LANES)
                h_buf[l, ct, HEAD_C:HEAD_C + T, :] = hgl[:, lanes]
                acc = None
                for j in range(CONF_WIDTH):
                    term = (w_dw_c_ref[l, j:j + 1, lanes]
                            * shifted(h_buf, (l, ct), HEAD_C, CONF_WIDTH - 1 - j))
                    acc = term if acc is None else acc + term
                carry(h_buf, (l, ct), HEAD_C)
                accs.append(acc)
            hc = jnp.concatenate(accs, axis=-1) + vec(l, b_dw_c_ref)
            hc = _silu(_layer_norm(hc, vec(l, ln_g_c_ref), vec(l, ln_b_c_ref)))
            y_c = dot_by_halves(hc.astype(jnp.bfloat16), w_pw2_ref[l])
            return (y_c + vec(l, b_pw2_c_ref)) * _silu_of_half(c_gh)

        def sgu(d_u, d_v, d_gh):
            u = _gelu_tanh(d_u)
            v = _layer_norm(_gelu_tanh(d_v), vec(l, ln_g_d_ref), vec(l, ln_b_d_ref))
            mixed_chunks = []
            for ck in range(T // SGU_CHUNK):
                v_ck = v[ck * SGU_CHUNK:(ck + 1) * SGU_CHUNK, :]
                stacked = jnp.concatenate([jnp.where(head_of_lane == hd, v_ck, 0.0).astype(jnp.bfloat16)
                                           for hd in range(N_HEADS_D)], axis=0)
                mixed_chunks.append(jnp.dot(w_s_ref[l], stacked, preferred_element_type=jnp.float32)
                                    + b_s_ref[l])
            mixed = jnp.concatenate(mixed_chunks, axis=0)
            return (u * mixed) * _silu_of_half(d_gh)

        z = dot_by_halves(h, w_in_ref[l])
        zs = [z[:, s * GW:(s + 1) * GW] for s in (Z_GROUP_ORDER.index(k) for k in range(D_IN // GW))]
        y_c = conformer(*zs[6:9])
        y_a = short_conv(*zs[0:4])
        y_d = sgu(*zs[9:12])
        y_b = pooling(*zs[4:6])
        y_cat = jnp.concatenate([y_a, y_b, y_c, y_d], axis=-1).astype(jnp.bfloat16)
        return x_cur + gate * dot_by_halves(y_cat, w_out_ref[l])

    x_cur = x_ref[...]
    for l in range(DEPTH):
        x_cur = layer(l, x_cur)
    o_ref[...] = _rms_norm(x_cur, final_g_ref[...])


def _const_spec(shape):
    zeros = (0,) * len(shape)
    return pl.BlockSpec(shape, lambda i: zeros, pipeline_mode=pl.Buffered(1))


def kernel(x, c, norm_g, w_ada, b_ada, w_in, w_conv_a, w_pool, pool_scale, w_dw_c, b_dw_c, ln_g_c, ln_b_c, w_pw2_c, b_pw2_c, ln_g_d, ln_b_d, w_s_d, b_s_d, w_out, final_g):
    f32 = jnp.float32
    assert x.shape == (1, SEQ, D_MODEL) and c.shape == (1, D_MODEL)
    assert POOL_WINDOWS == (2, 4, 8, 16) and N_CT == 2

    n_grp = len(POOL_WINDOWS)
    w_pool_bd = (w_pool[:, :, :, None, :] * jnp.eye(n_grp, dtype=f32)[None, :, None, :, None]
                 ).reshape(DEPTH, GW, GW).astype(jnp.bfloat16)
    b_s_full = jnp.broadcast_to(jnp.swapaxes(b_s_d, 1, 2)[:, :, :, None],
                                (DEPTH, SGU_CHUNK, N_HEADS_D, SGU_HEAD_DIM)).reshape(DEPTH, SGU_CHUNK, GW)
    vec_specs = [_const_spec((DEPTH, GW))] * 7

    T = ROW_TILE

    def row_tile(g):
        return (jnp.maximum(g - N_PREP, 0), 0)

    def weight_chunk(g):
        s = jnp.minimum(g, N_PREP - 1)
        return (s // PREP_CHUNKS, 0, s % PREP_CHUNKS)

    out = pl.pallas_call(
        _trunk_kernel,
        grid=(N_PREP + SEQ // T,),
        in_specs=[
            pl.BlockSpec((T, D_MODEL), row_tile),
            _const_spec((1, D_MODEL)),
            pl.BlockSpec((1, D_MODEL, PREP_IN_COLS), weight_chunk),
            _const_spec((DEPTH, 3 * D_MODEL)),
            _const_spec((DEPTH, D_MODEL)),
            _const_spec((1, D_MODEL)),
            pl.BlockSpec((1, D_MODEL, PREP_IN_COLS), weight_chunk),
            pl.BlockSpec((1, D_MODEL, PREP_OUT_COLS), weight_chunk),
            _const_spec((DEPTH, SHORT_CONV, GW)),
            _const_spec((DEPTH, CONF_WIDTH, GW)),
            *vec_specs,
            _const_spec((DEPTH, GW, GW)),
            _const_spec((DEPTH, GW, GW)),
            _const_spec((DEPTH, N_HEADS_D, SGU_CHUNK, SGU_CHUNK)),
            _const_spec((DEPTH, SGU_CHUNK, GW)),
        ],
        out_specs=pl.BlockSpec((T, D_MODEL), row_tile),
        out_shape=jax.ShapeDtypeStruct((SEQ, D_MODEL), f32),
        scratch_shapes=[
            pltpu.VMEM((DEPTH, N_CT, HEAD_A + T, LANES), f32),
            pltpu.VMEM((DEPTH, N_CT, HEAD_P + T, LANES), f32),
            pltpu.VMEM((DEPTH, N_CT, HEAD_P + T, LANES), f32),
            pltpu.VMEM((DEPTH, HEAD_P + T, LANES), f32),
            pltpu.VMEM((DEPTH, HEAD_P + T, LANES), f32),
            pltpu.VMEM((DEPTH, N_CT, HEAD_C + T, LANES), f32),
            pltpu.VMEM((DEPTH, SUBLANES, 3 * D_MODEL), f32),
            pltpu.VMEM((DEPTH, D_MODEL, D_IN), jnp.bfloat16),
            pltpu.VMEM((DEPTH, D_MODEL, D_MODEL), jnp.bfloat16),
            pltpu.VMEM((DEPTH, SGU_CHUNK, N_HEADS_D * SGU_CHUNK), jnp.bfloat16),
            pltpu.VMEM((DEPTH, GW, GW), jnp.bfloat16),
        ],
        compiler_params=pltpu.CompilerParams(
            dimension_semantics=("arbitrary",),
            vmem_limit_bytes=VMEM_LIMIT_BYTES),
        name="hybrid_trunk",
    )(x.reshape(SEQ, D_MODEL), c, w_ada, b_ada, norm_g, final_g.reshape(1, D_MODEL),
      w_in, w_out, w_conv_a, w_dw_c,
      pool_scale, b_dw_c, ln_g_c, ln_b_c, b_pw2_c, ln_g_d, ln_b_d,
      w_pool_bd, w_pw2_c, w_s_d, b_s_full)
    return out.reshape(1, SEQ, D_MODEL)
```

```python
import math

import jax
import jax.numpy as jnp
from jax import lax
from jax.experimental import pallas as pl
from jax.experimental.pallas import tpu as pltpu

D_MODEL = 1024
SEQ = 16384
DEPTH = 2
GW = 256
D_IN = 12 * GW
SHORT_CONV = 3
CONF_WIDTH = 31
POOL_WINDOWS = (2, 4, 8, 16)
POOL_GROUP_DIM = GW // len(POOL_WINDOWS)
SGU_CHUNK = 128
N_HEADS_D = 4
SGU_HEAD_DIM = GW // N_HEADS_D
EPS = 1e-6

SUBLANES = 8
LANES = 128
N_CT = GW // LANES
ROW_TILE = 512
HEAD_A = 8
HEAD_P = 8
HEAD_C = 32
VMEM_LIMIT_BYTES = 56 * 1024 * 1024
HALVED_GROUPS = (3, 5, 7, 8, 11)
PREP_CHUNKS = 4
N_PREP = DEPTH * PREP_CHUNKS
PREP_IN_COLS = D_IN // PREP_CHUNKS
PREP_OUT_COLS = D_MODEL // PREP_CHUNKS
Z_GROUP_ORDER = (6, 7, 1, 2, 0, 3, 10, 9, 11, 4, 5, 8)
LEAD_ROWS = ROW_TILE // 4


def _silu(x):
    hx = 0.5 * x
    return hx + hx * jnp.tanh(hx)


def _silu_of_half(hx):
    return hx + hx * jnp.tanh(hx)


def _sigmoid_of_half(hx):
    return 0.5 + 0.5 * jnp.tanh(hx)


def _gelu_tanh(x):
    c = math.sqrt(2.0 / math.pi)
    hx = 0.5 * x
    return hx + hx * jnp.tanh(x * ((x * x) * (0.044715 * c) + c))


def _layer_norm(x, g, b):
    mu = jnp.mean(x, axis=-1, keepdims=True)
    xc = x - mu
    var = jnp.mean(xc * xc, axis=-1, keepdims=True)
    return xc * lax.rsqrt(var + EPS) * g + b


def _rms_norm(x, g):
    ms = jnp.mean(x * x, axis=-1, keepdims=True)
    return (x * lax.rsqrt(ms + EPS)) * g


def _trunk_kernel(x_ref, c_ref, w_ada_ref, b_ada_ref, norm_g_ref, final_g_ref, w_in_ref, w_out_ref,
                  w_conv_a_ref, w_dw_c_ref, pool_scale_ref, b_dw_c_ref, ln_g_c_ref, ln_b_c_ref, b_pw2_c_ref,
                  ln_g_d_ref, ln_b_d_ref, w_pool_ref, w_pw2_ref, w_s_ref, b_s_ref, o_ref,
                  cx_buf, p_buf, s2_buf, s4_buf, s8_buf, h_buf, mod_buf, w_in_buf, w_out_buf, w_s_buf, w_pw2_buf):
    g = pl.program_id(0)
    T = ROW_TILE

    for s in range(N_PREP):
        l, j = divmod(s, PREP_CHUNKS)

        @pl.when(g == s)
        def _(l=l, j=j):
            in_cols = slice(j * PREP_IN_COLS, (j + 1) * PREP_IN_COLS)
            out_cols = slice(j * PREP_OUT_COLS, (j + 1) * PREP_OUT_COLS)
            for k in range(PREP_IN_COLS // GW):
                grp = j * PREP_IN_COLS // GW + k
                stored = Z_GROUP_ORDER.index(grp)
                w_grp = w_in_ref[0, :, k * GW:(k + 1) * GW]
                if grp in HALVED_GROUPS:
                    w_grp = w_grp * 0.5
                w_in_buf[l, :, stored * GW:(stored + 1) * GW] = w_grp.astype(jnp.bfloat16)
            w_out_buf[l, :, out_cols] = w_out_ref[0].astype(jnp.bfloat16)
            if j == 0:
                tri = (lax.broadcasted_iota(jnp.int32, (SGU_CHUNK, SGU_CHUNK), 0)
                       >= lax.broadcasted_iota(jnp.int32, (SGU_CHUNK, SGU_CHUNK), 1))
                w_pw2_buf[l] = w_pw2_ref[l].astype(jnp.bfloat16)
                w_s_buf[l] = jnp.concatenate([jnp.where(tri, w_s_ref[l, hd], 0.0).astype(jnp.bfloat16)
                                              for hd in range(N_HEADS_D)], axis=1)
            ca = _silu(jnp.broadcast_to(c_ref[...], (SUBLANES, D_MODEL))).astype(jnp.bfloat16)
            mod_buf[l, :, in_cols] = (jnp.dot(ca, w_ada_ref[0].astype(jnp.bfloat16),
                                              preferred_element_type=jnp.float32)
                                      + b_ada_ref[l:l + 1, in_cols])

    @pl.when(g == 0)
    def _():
        cx_buf[:, :, 0:HEAD_A, :] = jnp.zeros((DEPTH, N_CT, HEAD_A, LANES), jnp.float32)
        for buf in (p_buf, s2_buf):
            buf[:, :, 0:HEAD_P, :] = jnp.zeros((DEPTH, N_CT, HEAD_P, LANES), jnp.float32)
        for buf in (s4_buf, s8_buf):
            buf[:, 0:HEAD_P, :] = jnp.zeros((DEPTH, HEAD_P, LANES), jnp.float32)
        h_buf[:, :, 0:HEAD_C, :] = jnp.zeros((DEPTH, N_CT, HEAD_C, LANES), jnp.float32)

    @pl.when(g >= N_PREP)
    def _():
        _trunk_step(g - N_PREP, x_ref, mod_buf, norm_g_ref, final_g_ref, w_in_buf, w_out_buf,
                    w_conv_a_ref, w_dw_c_ref, pool_scale_ref, b_dw_c_ref, ln_g_c_ref, ln_b_c_ref, b_pw2_c_ref,
                    ln_g_d_ref, ln_b_d_ref, w_pool_ref, w_pw2_buf, w_s_buf, b_s_ref, o_ref,
                    cx_buf, p_buf, s2_buf, s4_buf, s8_buf, h_buf)


def _trunk_step(i, x_ref, mod_ref, norm_g_ref, final_g_ref, w_in_ref, w_out_ref,
                w_conv_a_ref, w_dw_c_ref, pool_scale_ref, b_dw_c_ref, ln_g_c_ref, ln_b_c_ref, b_pw2_c_ref,
                ln_g_d_ref, ln_b_d_ref, w_pool_ref, w_pw2_ref, w_s_ref, b_s_ref, o_ref,
                cx_buf, p_buf, s2_buf, s4_buf, s8_buf, h_buf):
    T = ROW_TILE
    lane = lax.broadcasted_iota(jnp.int32, (T, LANES), 1)
    left = lane < POOL_GROUP_DIM
    top = max(POOL_WINDOWS)
    row1 = lax.broadcasted_iota(jnp.int32, (top, LANES), 0) + (i * T + 1)
    inv_count = []
    left_top = lax.broadcasted_iota(jnp.int32, (top, LANES), 1) < POOL_GROUP_DIM
    left_row = lax.broadcasted_iota(jnp.int32, (1, LANES), 1) < POOL_GROUP_DIM
    for ct in range(N_CT):
        w_left, w_right = POOL_WINDOWS[2 * ct], POOL_WINDOWS[2 * ct + 1]
        window = jnp.where(left_top, w_left, w_right)
        inv_count.append((1.0 / jnp.minimum(row1, window).astype(jnp.float32),
                          jnp.where(left_row, 1.0 / w_left, 1.0 / w_right)))
    head_of_lane = lax.broadcasted_iota(jnp.int32, (SGU_CHUNK, GW), 1) // SGU_HEAD_DIM

    def vec(l, ref):
        return ref[l:l + 1, :]

    def shifted(buf, idx, head, back):
        return buf[idx + (slice(head - back, head - back + T), slice(None))]

    def carry(buf, idx, head):
        buf[idx + (slice(0, head), slice(None))] = buf[idx + (slice(T, T + head), slice(None))]

    def dot_row_split(lhs, rhs):
        return jnp.concatenate([jnp.dot(lhs[lo:hi], rhs, preferred_element_type=jnp.float32)
                                for lo, hi in ((0, LEAD_ROWS), (LEAD_ROWS, T))], axis=0)

    def layer(l, x_cur):
        shift = mod_ref[l, 0:1, 0:D_MODEL]
        scale = mod_ref[l, 0:1, D_MODEL:2 * D_MODEL]
        gate = mod_ref[l, 0:1, 2 * D_MODEL:3 * D_MODEL]
        h = (_rms_norm(x_cur, norm_g_ref[l:l + 1, :] * (1.0 + scale)) + shift).astype(jnp.bfloat16)

        def short_conv(a_b, a_c, a_x, a_gh):
            cx = a_c * a_x
            convs = []
            for ct in range(N_CT):
                lanes = slice(ct * LANES, (ct + 1) * LANES)
                cx_buf[l, ct, HEAD_A:HEAD_A + T, :] = cx[:, lanes]
                conv = None
                for j in range(SHORT_CONV):
                    term = (w_conv_a_ref[l, j:j + 1, lanes]
                            * shifted(cx_buf, (l, ct), HEAD_A, SHORT_CONV - 1 - j))
                    conv = term if conv is None else conv + term
                carry(cx_buf, (l, ct), HEAD_A)
                convs.append(conv)
            return (a_b * jnp.concatenate(convs, axis=-1)) * _silu_of_half(a_gh)

        def pooling(b_p, b_gh):
            pooled = []
            for ct in range(N_CT):
                p = b_p[:, ct * LANES:(ct + 1) * LANES]
                p_buf[l, ct, HEAD_P:HEAD_P + T, :] = p
                s2 = p + shifted(p_buf, (l, ct), HEAD_P, 1)
                s2_buf[l, ct, HEAD_P:HEAD_P + T, :] = s2
                s4 = s2 + shifted(s2_buf, (l, ct), HEAD_P, 2)
                carry(p_buf, (l, ct), HEAD_P)
                carry(s2_buf, (l, ct), HEAD_P)
                if ct == 0:
                    wsum = jnp.where(left, s2, s4)
                else:
                    s4_buf[l, HEAD_P:HEAD_P + T, :] = s4
                    s8 = s4 + shifted(s4_buf, (l,), HEAD_P, 4)
                    s8_buf[l, HEAD_P:HEAD_P + T, :] = s8
                    s16 = s8 + shifted(s8_buf, (l,), HEAD_P, 8)
                    carry(s4_buf, (l,), HEAD_P)
                    carry(s8_buf, (l,), HEAD_P)
                    wsum = jnp.where(left, s8, s16)
                inv_top, inv_rest = inv_count[ct]
                pooled.append(jnp.concatenate([wsum[0:top] * inv_top - p[0:top],
                                               wsum[top:] * inv_rest - p[top:]], axis=0))
            pooled = jnp.concatenate(pooled, axis=-1)
            y_b = dot_row_split(pooled.astype(jnp.bfloat16), w_pool_ref[l])
            return (y_b * vec(l, pool_scale_ref)) * _silu_of_half(b_gh)

        def conformer(c_a, c_glh, c_gh):
            hgl = c_a * _sigmoid_of_half(c_glh)
            accs = []
            for ct in range(N_CT):
                lanes = slice(ct * LANES, (ct + 1) * LANES)
                h_buf[l, ct, HEAD_C:HEAD_C + T, :] = hgl[:, lanes]
                acc = None
                for j in range(CONF_WIDTH):
                    term = (w_dw_c_ref[l, j:j + 1, lanes]
                            * shifted(h_buf, (l, ct), HEAD_C, CONF_WIDTH - 1 - j))
                    acc = term if acc is None else acc + term
                carry(h_buf, (l, ct), HEAD_C)
                accs.append(acc)
            hc = jnp.concatenate(accs, axis=-1) + vec(l, b_dw_c_ref)
            hc = _silu(_layer_norm(hc, vec(l, ln_g_c_ref), vec(l, ln_b_c_ref)))
            y_c = dot_row_split(hc.astype(jnp.bfloat16), w_pw2_ref[l])
            return (y_c + vec(l, b_pw2_c_ref)) * _silu_of_half(c_gh)

        def sgu(d_u, d_v, d_gh):
            u = _gelu_tanh(d_u)
            v = _layer_norm(_gelu_tanh(d_v), vec(l, ln_g_d_ref), vec(l, ln_b_d_ref))
            mixed_chunks = []
            for ck in range(T // SGU_CHUNK):
                v_ck = v[ck * SGU_CHUNK:(ck + 1) * SGU_CHUNK, :]
                stacked = jnp.concatenate([jnp.where(head_of_lane == hd, v_ck, 0.0).astype(jnp.bfloat16)
                                           for hd in range(N_HEADS_D)], axis=0)
                mixed_chunks.append(jnp.dot(w_s_ref[l], stacked, preferred_element_type=jnp.float32)
                                    + b_s_ref[l])
            mixed = jnp.concatenate(mixed_chunks, axis=0)
            return (u * mixed) * _silu_of_half(d_gh)

        z = dot_row_split(h, w_in_ref[l])
        zs = [z[:, s * GW:(s + 1) * GW] for s in (Z_GROUP_ORDER.index(k) for k in range(D_IN // GW))]
        y_c = conformer(*zs[6:9])
        y_a = short_conv(*zs[0:4])
        y_d = sgu(*zs[9:12])
        y_b = pooling(*zs[4:6])
        y_cat = jnp.concatenate([y_a, y_b, y_c, y_d], axis=-1).astype(jnp.bfloat16)
        return x_cur + gate * dot_row_split(y_cat, w_out_ref[l])

    x_cur = x_ref[...]
    for l in range(DEPTH):
        x_cur = layer(l, x_cur)
    o_ref[...] = _rms_norm(x_cur, final_g_ref[...])


def _const_spec(shape):
    zeros = (0,) * len(shape)
    return pl.BlockSpec(shape, lambda i: zeros, pipeline_mode=pl.Buffered(1))


def kernel(x, c, norm_g, w_ada, b_ada, w_in, w_conv_a, w_pool, pool_scale, w_dw_c, b_dw_c, ln_g_c, ln_b_c, w_pw2_c, b_pw2_c, ln_g_d, ln_b_d, w_s_d, b_s_d, w_out, final_g):
    f32 = jnp.float32
    assert x.shape == (1, SEQ, D_MODEL) and c.shape == (1, D_MODEL)
    assert POOL_WINDOWS == (2, 4, 8, 16) and N_CT == 2

    n_grp = len(POOL_WINDOWS)
    w_pool_bd = (w_pool[:, :, :, None, :] * jnp.eye(n_grp, dtype=f32)[None, :, None, :, None]
                 ).reshape(DEPTH, GW, GW).astype(jnp.bfloat16)
    b_s_full = jnp.broadcast_to(jnp.swapaxes(b_s_d, 1, 2)[:, :, :, None],
                                (DEPTH, SGU_CHUNK, N_HEADS_D, SGU_HEAD_DIM)).reshape(DEPTH, SGU_CHUNK, GW)
    vec_specs = [_const_spec((DEPTH, GW))] * 7

    T = ROW_TILE

    def row_tile(g):
        return (jnp.maximum(g - N_PREP, 0), 0)

    def weight_chunk(g):
        s = jnp.minimum(g, N_PREP - 1)
        return (s // PREP_CHUNKS, 0, s % PREP_CHUNKS)

    out = pl.pallas_call(
        _trunk_kernel,
        grid=(N_PREP + SEQ // T,),
        in_specs=[
            pl.BlockSpec((T, D_MODEL), row_tile),
            _const_spec((1, D_MODEL)),
            pl.BlockSpec((1, D_MODEL, PREP_IN_COLS), weight_chunk),
            _const_spec((DEPTH, 3 * D_MODEL)),
            _const_spec((DEPTH, D_MODEL)),
            _const_spec((1, D_MODEL)),
            pl.BlockSpec((1, D_MODEL, PREP_IN_COLS), weight_chunk),
            pl.BlockSpec((1, D_MODEL, PREP_OUT_COLS), weight_chunk),
            _const_spec((DEPTH, SHORT_CONV, GW)),
            _const_spec((DEPTH, CONF_WIDTH, GW)),
            *vec_specs,
            _const_spec((DEPTH, GW, GW)),
            _const_spec((DEPTH, GW, GW)),
            _const_spec((DEPTH, N_HEADS_D, SGU_CHUNK, SGU_CHUNK)),
            _const_spec((DEPTH, SGU_CHUNK, GW)),
        ],
        out_specs=pl.BlockSpec((T, D_MODEL), row_tile),
        out_shape=jax.ShapeDtypeStruct((SEQ, D_MODEL), f32),
        scratch_shapes=[
            pltpu.VMEM((DEPTH, N_CT, HEAD_A + T, LANES), f32),
            pltpu.VMEM((DEPTH, N_CT, HEAD_P + T, LANES), f32),
            pltpu.VMEM((DEPTH, N_CT, HEAD_P + T, LANES), f32),
            pltpu.VMEM((DEPTH, HEAD_P + T, LANES), f32),
            pltpu.VMEM((DEPTH, HEAD_P + T, LANES), f32),
            pltpu.VMEM((DEPTH, N_CT, HEAD_C + T, LANES), f32),
            pltpu.VMEM((DEPTH, SUBLANES, 3 * D_MODEL), f32),
            pltpu.VMEM((DEPTH, D_MODEL, D_IN), jnp.bfloat16),
            pltpu.VMEM((DEPTH, D_MODEL, D_MODEL), jnp.bfloat16),
            pltpu.VMEM((DEPTH, SGU_CHUNK, N_HEADS_D * SGU_CHUNK), jnp.bfloat16),
            pltpu.VMEM((DEPTH, GW, GW), jnp.bfloat16),
        ],
        compiler_params=pltpu.CompilerParams(
            dimension_semantics=("arbitrary",),
            vmem_limit_bytes=VMEM_LIMIT_BYTES),
        name="hybrid_trunk",
    )(x.reshape(SEQ, D_MODEL), c, w_ada, b_ada, norm_g, final_g.reshape(1, D_MODEL),
      w_in, w_out, w_conv_a, w_dw_c,
      pool_scale, b_dw_c, ln_g_c, ln_b_c, b_pw2_c, ln_g_d, ln_b_d,
      w_pool_bd, w_pw2_c, w_s_d, b_s_full)
    return out.reshape(1, SEQ, D_MODEL)
```

```python
import math

import jax
import jax.numpy as jnp
from jax import lax
from jax.experimental import pallas as pl
from jax.experimental.pallas import tpu as pltpu

D_MODEL = 1024
SEQ = 16384
DEPTH = 2
GW = 256
D_IN = 12 * GW
SHORT_CONV = 3
CONF_WIDTH = 31
POOL_WINDOWS = (2, 4, 8, 16)
POOL_GROUP_DIM = GW // len(POOL_WINDOWS)
SGU_CHUNK = 128
N_HEADS_D = 4
SGU_HEAD_DIM = GW // N_HEADS_D
EPS = 1e-6

SUBLANES = 8
LANES = 128
N_CT = GW // LANES
ROW_TILE = 512
HEAD_A = 8
HEAD_P = 8
HEAD_C = 32
VMEM_LIMIT_BYTES = 60 * 1024 * 1024
HALVED_GROUPS = (3, 5, 7, 8, 11)
PREP_CHUNKS = 4
N_PREP = DEPTH * PREP_CHUNKS
PREP_IN_COLS = D_IN // PREP_CHUNKS
PREP_OUT_COLS = D_MODEL // PREP_CHUNKS
Z_GROUP_ORDER = (6, 7, 1, 2, 0, 3, 10, 9, 11, 4, 5, 8)
X_SLOTS = 3
LEAD_ROWS = ROW_TILE // 4


def _silu(x):
    hx = 0.5 * x
    return hx + hx * jnp.tanh(hx)


def _silu_of_half(hx):
    return hx + hx * jnp.tanh(hx)


def _sigmoid_of_half(hx):
    return 0.5 + 0.5 * jnp.tanh(hx)


def _gelu_tanh(x):
    c = math.sqrt(2.0 / math.pi)
    hx = 0.5 * x
    return hx + hx * jnp.tanh(x * ((x * x) * (0.044715 * c) + c))


def _layer_norm(x, g, b):
    mu = jnp.mean(x, axis=-1, keepdims=True)
    xc = x - mu
    var = jnp.mean(xc * xc, axis=-1, keepdims=True)
    return xc * lax.rsqrt(var + EPS) * g + b


def _rms_norm(x, g):
    ms = jnp.mean(x * x, axis=-1, keepdims=True)
    return (x * lax.rsqrt(ms + EPS)) * g


def _trunk_kernel(x_ref, c_ref, w_ada_ref, b_ada_ref, norm_g_ref, final_g_ref, w_in_ref, w_out_ref,
                  w_conv_a_ref, w_dw_c_ref, pool_scale_ref, b_dw_c_ref, ln_g_c_ref, ln_b_c_ref, b_pw2_c_ref,
                  ln_g_d_ref, ln_b_d_ref, w_pool_ref, w_pw2_ref, w_s_ref, b_s_ref, o_ref,
                  cx_buf, p_buf, s2_buf, s4_buf, s8_buf, h_buf, mod_buf, w_in_buf, w_out_buf, w_s_buf, w_pw2_buf,
                  x_ring, x_sem):
    g = pl.program_id(0)
    T = ROW_TILE

    for s in range(N_PREP):
        l, j = divmod(s, PREP_CHUNKS)

        @pl.when(g == s)
        def _(l=l, j=j):
            in_cols = slice(j * PREP_IN_COLS, (j + 1) * PREP_IN_COLS)
            out_cols = slice(j * PREP_OUT_COLS, (j + 1) * PREP_OUT_COLS)
            for k in range(PREP_IN_COLS // GW):
                grp = j * PREP_IN_COLS // GW + k
                stored = Z_GROUP_ORDER.index(grp)
                w_grp = w_in_ref[0, :, k * GW:(k + 1) * GW]
                if grp in HALVED_GROUPS:
                    w_grp = w_grp * 0.5
                w_in_buf[l, :, stored * GW:(stored + 1) * GW] = w_grp.astype(jnp.bfloat16)
            w_out_buf[l, :, out_cols] = w_out_ref[0].astype(jnp.bfloat16)
            if j == 0:
                tri = (lax.broadcasted_iota(jnp.int32, (SGU_CHUNK, SGU_CHUNK), 0)
                       >= lax.broadcasted_iota(jnp.int32, (SGU_CHUNK, SGU_CHUNK), 1))
                w_pw2_buf[l] = w_pw2_ref[l].astype(jnp.bfloat16)
                w_s_buf[l] = jnp.concatenate([jnp.where(tri, w_s_ref[l, hd], 0.0).astype(jnp.bfloat16)
                                              for hd in range(N_HEADS_D)], axis=1)
            ca = _silu(jnp.broadcast_to(c_ref[...], (SUBLANES, D_MODEL))).astype(jnp.bfloat16)
            mod_buf[l, :, in_cols] = (jnp.dot(ca, w_ada_ref[0].astype(jnp.bfloat16),
                                              preferred_element_type=jnp.float32)
                                      + b_ada_ref[l:l + 1, in_cols])

    @pl.when(g == 0)
    def _():
        cx_buf[:, :, 0:HEAD_A, :] = jnp.zeros((DEPTH, N_CT, HEAD_A, LANES), jnp.float32)
        for buf in (p_buf, s2_buf):
            buf[:, :, 0:HEAD_P, :] = jnp.zeros((DEPTH, N_CT, HEAD_P, LANES), jnp.float32)
        for buf in (s4_buf, s8_buf):
            buf[:, 0:HEAD_P, :] = jnp.zeros((DEPTH, HEAD_P, LANES), jnp.float32)
        h_buf[:, :, 0:HEAD_C, :] = jnp.zeros((DEPTH, N_CT, HEAD_C, LANES), jnp.float32)

    n_tiles = SEQ // T

    def x_copy(tile, slot):
        return pltpu.make_async_copy(x_ref.at[pl.ds(tile * T, T), :], x_ring.at[slot], x_sem.at[slot])

    @pl.when(g == 0)
    def _():
        for tile in range(X_SLOTS - 1):
            x_copy(tile, tile).start()

    @pl.when(g >= N_PREP)
    def _():
        i = g - N_PREP
        slot = lax.rem(i, X_SLOTS)
        x_copy(i, slot).wait()

        @pl.when(i + X_SLOTS - 1 < n_tiles)
        def _():
            x_copy(i + X_SLOTS - 1, lax.rem(i + X_SLOTS - 1, X_SLOTS)).start()

        _trunk_step(i, x_ring.at[slot], mod_buf, norm_g_ref, final_g_ref, w_in_buf, w_out_buf,
                    w_conv_a_ref, w_dw_c_ref, pool_scale_ref, b_dw_c_ref, ln_g_c_ref, ln_b_c_ref, b_pw2_c_ref,
                    ln_g_d_ref, ln_b_d_ref, w_pool_ref, w_pw2_buf, w_s_buf, b_s_ref, o_ref,
                    cx_buf, p_buf, s2_buf, s4_buf, s8_buf, h_buf)


def _trunk_step(i, x_ref, mod_ref, norm_g_ref, final_g_ref, w_in_ref, w_out_ref,
                w_conv_a_ref, w_dw_c_ref, pool_scale_ref, b_dw_c_ref, ln_g_c_ref, ln_b_c_ref, b_pw2_c_ref,
                ln_g_d_ref, ln_b_d_ref, w_pool_ref, w_pw2_ref, w_s_ref, b_s_ref, o_ref,
                cx_buf, p_buf, s2_buf, s4_buf, s8_buf, h_buf):
    T = ROW_TILE
    lane = lax.broadcasted_iota(jnp.int32, (T, LANES), 1)
    left = lane < POOL_GROUP_DIM
    top = max(POOL_WINDOWS)
    row1 = lax.broadcasted_iota(jnp.int32, (top, LANES), 0) + (i * T + 1)
    inv_count = []
    left_top = lax.broadcasted_iota(jnp.int32, (top, LANES), 1) < POOL_GROUP_DIM
    left_row = lax.broadcasted_iota(jnp.int32, (1, LANES), 1) < POOL_GROUP_DIM
    for ct in range(N_CT):
        w_left, w_right = POOL_WINDOWS[2 * ct], POOL_WINDOWS[2 * ct + 1]
        window = jnp.where(left_top, w_left, w_right)
        inv_count.append((1.0 / jnp.minimum(row1, window).astype(jnp.float32),
                          jnp.where(left_row, 1.0 / w_left, 1.0 / w_right)))
    head_of_lane = lax.broadcasted_iota(jnp.int32, (SGU_CHUNK, GW), 1) // SGU_HEAD_DIM

    def vec(l, ref):
        return ref[l:l + 1, :]

    def shifted(buf, idx, head, back):
        return buf[idx + (slice(head - back, head - back + T), slice(None))]

    def carry(buf, idx, head):
        buf[idx + (slice(0, head), slice(None))] = buf[idx + (slice(T, T + head), slice(None))]

    def dot_row_split(lhs, rhs):
        return jnp.concatenate([jnp.dot(lhs[lo:hi], rhs, preferred_element_type=jnp.float32)
                                for lo, hi in ((0, LEAD_ROWS), (LEAD_ROWS, T))], axis=0)

    def layer(l, x_cur):
        shift = mod_ref[l, 0:1, 0:D_MODEL]
        scale = mod_ref[l, 0:1, D_MODEL:2 * D_MODEL]
        gate = mod_ref[l, 0:1, 2 * D_MODEL:3 * D_MODEL]
        h = (_rms_norm(x_cur, norm_g_ref[l:l + 1, :] * (1.0 + scale)) + shift).astype(jnp.bfloat16)

        def short_conv(a_b, a_c, a_x, a_gh):
            cx = a_c * a_x
            convs = []
            for ct in range(N_CT):
                lanes = slice(ct * LANES, (ct + 1) * LANES)
                cx_buf[l, ct, HEAD_A:HEAD_A + T, :] = cx[:, lanes]
                conv = None
                for j in range(SHORT_CONV):
                    term = (w_conv_a_ref[l, j:j + 1, lanes]
                            * shifted(cx_buf, (l, ct), HEAD_A, SHORT_CONV - 1 - j))
                    conv = term if conv is None else conv + term
                carry(cx_buf, (l, ct), HEAD_A)
                convs.append(conv)
            return (a_b * jnp.concatenate(convs, axis=-1)) * _silu_of_half(a_gh)

        def pooling(b_p, b_gh):
            pooled = []
            for ct in range(N_CT):
                p = b_p[:, ct * LANES:(ct + 1) * LANES]
                p_buf[l, ct, HEAD_P:HEAD_P + T, :] = p
                s2 = p + shifted(p_buf, (l, ct), HEAD_P, 1)
                s2_buf[l, ct, HEAD_P:HEAD_P + T, :] = s2
                s4 = s2 + shifted(s2_buf, (l, ct), HEAD_P, 2)
                carry(p_buf, (l, ct), HEAD_P)
                carry(s2_buf, (l, ct), HEAD_P)
                if ct == 0:
                    wsum = jnp.where(left, s2, s4)
                else:
                    s4_buf[l, HEAD_P:HEAD_P + T, :] = s4
                    s8 = s4 + shifted(s4_buf, (l,), HEAD_P, 4)
                    s8_buf[l, HEAD_P:HEAD_P + T, :] = s8
                    s16 = s8 + shifted(s8_buf, (l,), HEAD_P, 8)
                    carry(s4_buf, (l,), HEAD_P)
                    carry(s8_buf, (l,), HEAD_P)
                    wsum = jnp.where(left, s8, s16)
                inv_top, inv_rest = inv_count[ct]
                pooled.append(jnp.concatenate([wsum[0:top] * inv_top - p[0:top],
                                               wsum[top:] * inv_rest - p[top:]], axis=0))
            pooled = jnp.concatenate(pooled, axis=-1)
            y_b = dot_row_split(pooled.astype(jnp.bfloat16), w_pool_ref[l])
            return (y_b * vec(l, pool_scale_ref)) * _silu_of_half(b_gh)

        def conformer(c_a, c_glh, c_gh):
            hgl = c_a * _sigmoid_of_half(c_glh)
            accs = []
            for ct in range(N_CT):
                lanes = slice(ct * LANES, (ct + 1) * LANES)
                h_buf[l, ct, HEAD_C:HEAD_C + T, :] = hgl[:, lanes]
                acc = None
                for j in range(CONF_WIDTH):
                    term = (w_dw_c_ref[l, j:j + 1, lanes]
                            * shifted(h_buf, (l, ct), HEAD_C, CONF_WIDTH - 1 - j))
                    acc = term if acc is None else acc + term
                carry(h_buf, (l, ct), HEAD_C)
                accs.append(acc)
            hc = jnp.concatenate(accs, axis=-1) + vec(l, b_dw_c_ref)
            hc = _silu(_layer_norm(hc, vec(l, ln_g_c_ref), vec(l, ln_b_c_ref)))
            y_c = dot_row_split(hc.astype(jnp.bfloat16), w_pw2_ref[l])
            return (y_c + vec(l, b_pw2_c_ref)) * _silu_of_half(c_gh)

        def sgu(d_u, d_v, d_gh):
            u = _gelu_tanh(d_u)
            v = _layer_norm(_gelu_tanh(d_v), vec(l, ln_g_d_ref), vec(l, ln_b_d_ref))
            mixed_chunks = []
            for ck in range(T // SGU_CHUNK):
                v_ck = v[ck * SGU_CHUNK:(ck + 1) * SGU_CHUNK, :]
                stacked = jnp.concatenate([jnp.where(head_of_lane == hd, v_ck, 0.0).astype(jnp.bfloat16)
                                           for hd in range(N_HEADS_D)], axis=0)
                mixed_chunks.append(jnp.dot(w_s_ref[l], stacked, preferred_element_type=jnp.float32)
                                    + b_s_ref[l])
            mixed = jnp.concatenate(mixed_chunks, axis=0)
            return (u * mixed) * _silu_of_half(d_gh)

        z = dot_row_split(h, w_in_ref[l])
        zs = [z[:, s * GW:(s + 1) * GW] for s in (Z_GROUP_ORDER.index(k) for k in range(D_IN // GW))]
        y_c = conformer(*zs[6:9])
        y_a = short_conv(*zs[0:4])
        y_d = sgu(*zs[9:12])
        y_b = pooling(*zs[4:6])
        y_cat = jnp.concatenate([y_a, y_b, y_c, y_d], axis=-1).astype(jnp.bfloat16)
        return x_cur + gate * dot_row_split(y_cat, w_out_ref[l])

    x_cur = x_ref[...]
    for l in range(DEPTH):
        x_cur = layer(l, x_cur)
    o_ref[...] = _rms_norm(x_cur, final_g_ref[...])


def _const_spec(shape):
    zeros = (0,) * len(shape)
    return pl.BlockSpec(shape, lambda i: zeros, pipeline_mode=pl.Buffered(1))


def kernel(x, c, norm_g, w_ada, b_ada, w_in, w_conv_a, w_pool, pool_scale, w_dw_c, b_dw_c, ln_g_c, ln_b_c, w_pw2_c, b_pw2_c, ln_g_d, ln_b_d, w_s_d, b_s_d, w_out, final_g):
    f32 = jnp.float32
    assert x.shape == (1, SEQ, D_MODEL) and c.shape == (1, D_MODEL)
    assert POOL_WINDOWS == (2, 4, 8, 16) and N_CT == 2

    n_grp = len(POOL_WINDOWS)
    w_pool_bd = (w_pool[:, :, :, None, :] * jnp.eye(n_grp, dtype=f32)[None, :, None, :, None]
                 ).reshape(DEPTH, GW, GW).astype(jnp.bfloat16)
    b_s_full = jnp.broadcast_to(jnp.swapaxes(b_s_d, 1, 2)[:, :, :, None],
                                (DEPTH, SGU_CHUNK, N_HEADS_D, SGU_HEAD_DIM)).reshape(DEPTH, SGU_CHUNK, GW)
    vec_specs = [_const_spec((DEPTH, GW))] * 7

    T = ROW_TILE

    def row_tile(g):
        return (jnp.maximum(g - N_PREP, 0), 0)

    def weight_chunk(g):
        s = jnp.minimum(g, N_PREP - 1)
        return (s // PREP_CHUNKS, 0, s % PREP_CHUNKS)

    out = pl.pallas_call(
        _trunk_kernel,
        grid=(N_PREP + SEQ // T,),
        in_specs=[
            pl.BlockSpec(memory_space=pl.ANY),
            _const_spec((1, D_MODEL)),
            pl.BlockSpec((1, D_MODEL, PREP_IN_COLS), weight_chunk),
            _const_spec((DEPTH, 3 * D_MODEL)),
            _const_spec((DEPTH, D_MODEL)),
            _const_spec((1, D_MODEL)),
            pl.BlockSpec((1, D_MODEL, PREP_IN_COLS), weight_chunk),
            pl.BlockSpec((1, D_MODEL, PREP_OUT_COLS), weight_chunk),
            _const_spec((DEPTH, SHORT_CONV, GW)),
            _const_spec((DEPTH, CONF_WIDTH, GW)),
            *vec_specs,
            _const_spec((DEPTH, GW, GW)),
            _const_spec((DEPTH, GW, GW)),
            _const_spec((DEPTH, N_HEADS_D, SGU_CHUNK, SGU_CHUNK)),
            _const_spec((DEPTH, SGU_CHUNK, GW)),
        ],
        out_specs=pl.BlockSpec((T, D_MODEL), row_tile),
        out_shape=jax.ShapeDtypeStruct((SEQ, D_MODEL), f32),
        scratch_shapes=[
            pltpu.VMEM((DEPTH, N_CT, HEAD_A + T, LANES), f32),
            pltpu.VMEM((DEPTH, N_CT, HEAD_P + T, LANES), f32),
            pltpu.VMEM((DEPTH, N_CT, HEAD_P + T, LANES), f32),
            pltpu.VMEM((DEPTH, HEAD_P + T, LANES), f32),
            pltpu.VMEM((DEPTH, HEAD_P + T, LANES), f32),
            pltpu.VMEM((DEPTH, N_CT, HEAD_C + T, LANES), f32),
            pltpu.VMEM((DEPTH, SUBLANES, 3 * D_MODEL), f32),
            pltpu.VMEM((DEPTH, D_MODEL, D_IN), jnp.bfloat16),
            pltpu.VMEM((DEPTH, D_MODEL, D_MODEL), jnp.bfloat16),
            pltpu.VMEM((DEPTH, SGU_CHUNK, N_HEADS_D * SGU_CHUNK), jnp.bfloat16),
            pltpu.VMEM((DEPTH, GW, GW), jnp.bfloat16),
            pltpu.VMEM((X_SLOTS, T, D_MODEL), f32),
            pltpu.SemaphoreType.DMA((X_SLOTS,)),
        ],
        compiler_params=pltpu.CompilerParams(
            dimension_semantics=("arbitrary",),
            vmem_limit_bytes=VMEM_LIMIT_BYTES),
        name="hybrid_trunk",
    )(x.reshape(SEQ, D_MODEL), c, w_ada, b_ada, norm_g, final_g.reshape(1, D_MODEL),
      w_in, w_out, w_conv_a, w_dw_c,
      pool_scale, b_dw_c, ln_g_c, ln_b_c, b_pw2_c, ln_g_d, ln_b_d,
      w_pool_bd, w_pw2_c, w_s_d, b_s_full)
    return out.reshape(1, SEQ, D_MODEL)
```
